```python
import math
import jax
import jax.numpy as jnp
from jax import lax
import numpy as np

D_MODEL = 2048
BATCH = 2
SEQ = 4096
DEPTH = 1
DEC_BATCH = 32
DEC_SEQ = 1
PAST_LEN = 8192
PAGE_SIZE = 128

MIX_WIDTH = D_MODEL
SB_HEADS = 8
SB_HEAD_DIM = MIX_WIDTH // 2 // SB_HEADS
DIFF_HEADS = 8
DIFF_V_DIM = MIX_WIDTH // 2 // DIFF_HEADS
DIFF_QK_DIM = DIFF_V_DIM // 2
ROT_DIM = DIFF_QK_DIM // 4
ROPE_THETA = 500000.0
Q_BLOCK = 128
SB_W = SB_HEADS * SB_HEAD_DIM
DIFF_QK_W = DIFF_HEADS * 2 * DIFF_QK_DIM
DIFF_V_W = DIFF_HEADS * DIFF_V_DIM
IN_SPLITS = (SB_W, SB_W, SB_W, DIFF_QK_W, DIFF_QK_W, DIFF_V_W)
IN_WIDTH = 3 * SB_W + 2 * DIFF_QK_W + DIFF_V_W
PEER_HEADS = 8
PEER_N_KEYS = 128
PEER_N_EXPERTS = PEER_N_KEYS * PEER_N_KEYS
PEER_QUERY_DIM = 256
PEER_HALF = PEER_QUERY_DIM // 2
PEER_TOPK = 16
PEER_TOKEN_BLOCK = 128
N_MOD = 6
EPS = 1e-6

kernel_name = "hymba_stickbreak_diffattn_peer_step"


def rmsnorm(x, g):
    xf = x.astype(jnp.float32)
    y = xf * lax.rsqrt(jnp.mean(xf * xf, axis=-1, keepdims=True) + EPS) * g.astype(jnp.float32)
    return y.astype(x.dtype)


def adaln(c, w, b, n):
    m = (jax.nn.silu(c) @ w + b).reshape(c.shape[0], 1, n, D_MODEL)
    return [m[:, :, i] for i in range(n)]


def modulate(h, shift, scale):
    return h * (1.0 + scale) + shift


def rope_partial(x, pos):
    inv = ROPE_THETA ** (-jnp.arange(0, ROT_DIM, 2, dtype=jnp.float32) / ROT_DIM)
    ang = pos.astype(jnp.float32)[:, None] * inv
    cos = jnp.cos(ang)[:, None, None, :]
    sin = jnp.sin(ang)[:, None, None, :]
    xf = x.astype(jnp.float32)
    x1 = xf[..., : ROT_DIM // 2]
    x2 = xf[..., ROT_DIM // 2: ROT_DIM]
    out = jnp.concatenate([x1 * cos - x2 * sin, x1 * sin + x2 * cos, xf[..., ROT_DIM:]], axis=-1)
    return out.astype(x.dtype)


def project(h, w_in, pos):
    n, t = h.shape[0], h.shape[1]
    z = h @ w_in
    offs, acc = [], 0
    for w in IN_SPLITS[:-1]:
        acc += w
        offs.append(acc)
    sb_q, sb_k, sb_v, d_q, d_k, d_v = jnp.split(z, offs, axis=-1)
    sb_q = sb_q.reshape(n, t, SB_HEADS, SB_HEAD_DIM)
    sb_k = sb_k.reshape(n, t, SB_HEADS, SB_HEAD_DIM)
    sb_v = sb_v.reshape(n, t, SB_HEADS, SB_HEAD_DIM)
    d_q = rope_partial(d_q.reshape(n, t, DIFF_HEADS, 2, DIFF_QK_DIM), pos)
    d_k = rope_partial(d_k.reshape(n, t, DIFF_HEADS, 2, DIFF_QK_DIM), pos)
    d_v = d_v.reshape(n, t, DIFF_HEADS, DIFF_V_DIM)
    return sb_q, sb_k, sb_v, d_q, d_k, d_v


def sb_attend(q, k, v, q_pos, k_pos):
    z = jnp.einsum('qhd,khd->hqk', q.astype(jnp.float32), k.astype(jnp.float32)) / math.sqrt(SB_HEAD_DIM)
    mask = k_pos[None, :] < q_pos[:, None]
    log_keep = jnp.where(mask, jax.nn.log_sigmoid(-z), 0.0)
    after = lax.cumsum(log_keep, axis=2, reverse=True) - log_keep
    a = jnp.where(mask, jnp.exp(jax.nn.log_sigmoid(z) + after), 0.0)
    return jnp.einsum('hqk,khd->qhd', a, v.astype(jnp.float32)).astype(q.dtype)


def diff_attend(q, k, v, q_pos, k_pos, lam):
    s = jnp.einsum('qhcd,khcd->chqk', q.astype(jnp.float32), k.astype(jnp.float32)) / math.sqrt(DIFF_QK_DIM)
    mask = k_pos[None, :] <= q_pos[:, None]
    p = jax.nn.softmax(jnp.where(mask, s, -jnp.inf), axis=-1)
    a = p[0] - lam * p[1]
    return jnp.einsum('hqk,khd->qhd', a, v.astype(jnp.float32)).astype(q.dtype)


def attend_both(sb_q, sb_k, sb_v, d_q, d_k, d_v, q_pos, k_pos, lam):
    return (sb_attend(sb_q, sb_k, sb_v, q_pos, k_pos),
            diff_attend(d_q, d_k, d_v, q_pos, k_pos, lam))


def prompt_attention(sb_q, sb_k, sb_v, d_q, d_k, d_v, lam):
    b, s = sb_q.shape[0], sb_q.shape[1]
    nb = s // Q_BLOCK
    k_pos = jnp.arange(s)

    def to_blocks(t):
        return jnp.moveaxis(t.reshape((b, nb, Q_BLOCK) + t.shape[2:]), 1, 0)

    def from_blocks(t):
        return jnp.moveaxis(t, 0, 1).reshape((b, s) + t.shape[3:])

    def one_block(args):
        sbq, dq, qpos = args
        f = lambda a1, a2, a3, a4, a5, a6: attend_both(a1, a2, a3, a4, a5, a6, qpos, k_pos, lam)
        return jax.vmap(f)(sbq, sb_k, sb_v, dq, d_k, d_v)

    sb_o, d_o = lax.map(one_block, (to_blocks(sb_q), to_blocks(d_q), k_pos.reshape(nb, Q_BLOCK)))
    return from_blocks(sb_o), from_blocks(d_o)


def sample_attention(sb_q, sb_k, sb_v, d_q, d_k, d_v, lam, layer,
                     cache_sb_k, cache_sb_v, cache_diff_k, cache_diff_v, page_table):
    ds = sb_q.shape[1]
    q_pos = PAST_LEN + jnp.arange(ds)
    k_pos = jnp.arange(PAST_LEN + ds)

    def gather(pool, pages, new):
        past = pool[layer, pages]
        past = past.reshape((-1,) + pool.shape[3:]).astype(new.dtype)
        return jnp.concatenate([past, new], axis=0)

    def one_seq(args):
        sbq, sbk, sbv, dq, dk, dv, pages = args
        return attend_both(sbq, gather(cache_sb_k, pages, sbk), gather(cache_sb_v, pages, sbv),
                           dq, gather(cache_diff_k, pages, dk), gather(cache_diff_v, pages, dv),
                           q_pos, k_pos, lam)

    return lax.map(one_seq, (sb_q, sb_k, sb_v, d_q, d_k, d_v, page_table))


def merge_heads(sb_o, d_o, diff_g, lam_init, w_out):
    n, t = sb_o.shape[0], sb_o.shape[1]
    d_n = rmsnorm(d_o, diff_g) * (1.0 - lam_init)
    o = jnp.concatenate([sb_o.reshape(n, t, SB_W), d_n.reshape(n, t, DIFF_V_W)], axis=-1)
    return o @ w_out


def peer(h, w_q, sub_keys, u_tab, v_tab):
    t = h.shape[0]
    q = (h @ w_q).reshape(t, PEER_HEADS, 2, PEER_HALF)
    s = jnp.einsum('thcd,hcnd->thcn', q.astype(jnp.float32), sub_keys.astype(jnp.float32))
    v_top, i_top = lax.top_k(s, PEER_TOPK)
    cand = (v_top[:, :, 0, :, None] + v_top[:, :, 1, None, :]).reshape(t, PEER_HEADS, PEER_TOPK * PEER_TOPK)
    cand_id = (i_top[:, :, 0, :, None] * PEER_N_KEYS + i_top[:, :, 1, None, :]).reshape(t, PEER_HEADS, PEER_TOPK * PEER_TOPK)
    best, sel = lax.top_k(cand, PEER_TOPK)
    ids = jnp.take_along_axis(cand_id, sel, axis=-1)
    g = jax.nn.softmax(best, axis=-1)
    u = u_tab[ids]
    act = jax.nn.gelu(jnp.einsum('thkd,td->thk', u.astype(jnp.float32), h.astype(jnp.float32)), approximate=False)
    w = (g * act).astype(h.dtype)
    return jnp.einsum('thk,thkd->td', w, v_tab[ids])


def peer_prompt(h, w_q, sub_keys, u_tab, v_tab):
    blocks = h.reshape(-1, PEER_TOKEN_BLOCK, D_MODEL)
    out = lax.map(lambda hb: peer(hb, w_q, sub_keys, u_tab, v_tab), blocks)
    return out.reshape(h.shape)


def setup_inputs(seed: int = 0) -> dict:
    key = jax.random.key(seed)
    ks = jax.random.split(key, 24)
    f32 = jnp.float32
    n_pages = PAST_LEN // PAGE_SIZE
    n_used = DEC_BATCH * n_pages
    n_pool = n_used + max(1, n_used // 4)

    def nrm(k, shape, s):
        return jax.random.normal(k, shape, f32) * s

    page_table = jax.random.permutation(ks[8], n_pool)[:n_used].reshape(DEC_BATCH, n_pages).astype(jnp.int32)
    return {
        "x_prompt": nrm(ks[0], (BATCH, SEQ, D_MODEL), 1.0),
        "x_sample": nrm(ks[1], (DEC_BATCH, DEC_SEQ, D_MODEL), 1.0),
        "c_prompt": nrm(ks[2], (BATCH, D_MODEL), 1.0),
        "c_sample": nrm(ks[3], (DEC_BATCH, D_MODEL), 1.0),
        "cache_sb_k": nrm(ks[4], (DEPTH, n_pool, PAGE_SIZE, SB_HEADS, SB_HEAD_DIM), 1.0),
        "cache_sb_v": nrm(ks[5], (DEPTH, n_pool, PAGE_SIZE, SB_HEADS, SB_HEAD_DIM), 1.0),
        "cache_diff_k": nrm(ks[6], (DEPTH, n_pool, PAGE_SIZE, DIFF_HEADS, 2, DIFF_QK_DIM), 1.0),
        "cache_diff_v": nrm(ks[7], (DEPTH, n_pool, PAGE_SIZE, DIFF_HEADS, DIFF_V_DIM), 1.0),
        "page_table": page_table,
        "norm_mix_g": 1.0 + nrm(ks[9], (DEPTH, D_MODEL), 0.02),
        "norm_ffn_g": 1.0 + nrm(ks[10], (DEPTH, D_MODEL), 0.02),
        "w_ada": nrm(ks[11], (DEPTH, D_MODEL, N_MOD * D_MODEL), 0.5 * D_MODEL ** -0.5),
        "b_ada": nrm(ks[12], (DEPTH, N_MOD * D_MODEL), 0.01),
        "w_in": nrm(ks[13], (DEPTH, D_MODEL, IN_WIDTH), D_MODEL ** -0.5),
        "diff_lambda": nrm(ks[14], (DEPTH, 4, DIFF_QK_DIM), 0.1),
        "diff_norm_g": 1.0 + nrm(ks[15], (DEPTH, DIFF_V_DIM), 0.02),
        "w_out": nrm(ks[16], (DEPTH, MIX_WIDTH, D_MODEL), MIX_WIDTH ** -0.5),
        "peer_w_q": nrm(ks[17], (DEPTH, D_MODEL, PEER_HEADS * PEER_QUERY_DIM), D_MODEL ** -0.5),
        "peer_sub_keys": nrm(ks[18], (DEPTH, PEER_HEADS, 2, PEER_N_KEYS, PEER_HALF), PEER_HALF ** -0.5),
        "peer_u": nrm(ks[19], (DEPTH, PEER_N_EXPERTS, D_MODEL), D_MODEL ** -0.5),
        "peer_v": nrm(ks[20], (DEPTH, PEER_N_EXPERTS, D_MODEL), 1.0),
        "norm_final_g": 1.0 + nrm(ks[21], (D_MODEL,), 0.02),
        "w_ada_final": nrm(ks[22], (D_MODEL, 2 * D_MODEL), 0.5 * D_MODEL ** -0.5),
        "b_ada_final": nrm(ks[23], (2 * D_MODEL,), 0.01),
    }


def reference(x_prompt, x_sample, c_prompt, c_sample, cache_sb_k, cache_sb_v, cache_diff_k, cache_diff_v,
              page_table, norm_mix_g, norm_ffn_g, w_ada, b_ada, w_in, diff_lambda, diff_norm_g, w_out,
              peer_w_q, peer_sub_keys, peer_u, peer_v, norm_final_g, w_ada_final, b_ada_final):
    pos_p = jnp.arange(x_prompt.shape[1])
    pos_s = PAST_LEN + jnp.arange(x_sample.shape[1])
    xp, xs = x_prompt, x_sample
    p_rows = ([], [], [], [])
    s_rows = ([], [], [], [])
    for l in range(DEPTH):
        lam_init = 0.8 - 0.6 * math.exp(-0.3 * l)
        lp = diff_lambda[l].astype(jnp.float32)
        lam = jnp.exp(jnp.sum(lp[0] * lp[1])) - jnp.exp(jnp.sum(lp[2] * lp[3])) + lam_init
        mp = adaln(c_prompt, w_ada[l], b_ada[l], N_MOD)
        ms = adaln(c_sample, w_ada[l], b_ada[l], N_MOD)
        qp = project(modulate(rmsnorm(xp, norm_mix_g[l]), mp[0], mp[1]), w_in[l], pos_p)
        qs = project(modulate(rmsnorm(xs, norm_mix_g[l]), ms[0], ms[1]), w_in[l], pos_s)
        sbo_p, do_p = prompt_attention(*qp, lam)
        sbo_s, do_s = sample_attention(*qs, lam, l, cache_sb_k, cache_sb_v, cache_diff_k, cache_diff_v, page_table)
        xp = xp + mp[2] * merge_heads(sbo_p, do_p, diff_norm_g[l], lam_init, w_out[l])
        xs = xs + ms[2] * merge_heads(sbo_s, do_s, diff_norm_g[l], lam_init, w_out[l])
        for i, j in enumerate((1, 2, 4, 5)):
            p_rows[i].append(qp[j])
            s_rows[i].append(qs[j])
        hp = modulate(rmsnorm(xp, norm_ffn_g[l]), mp[3], mp[4])
        hs = modulate(rmsnorm(xs, norm_ffn_g[l]), ms[3], ms[4])
        xp = xp + mp[5] * peer_prompt(hp, peer_w_q[l], peer_sub_keys[l], peer_u[l], peer_v[l])
        xs = xs + ms[5] * peer(hs.reshape(-1, D_MODEL), peer_w_q[l], peer_sub_keys[l], peer_u[l], peer_v[l]).reshape(xs.shape)
    fp = adaln(c_prompt, w_ada_final, b_ada_final, 2)
    fs = adaln(c_sample, w_ada_final, b_ada_final, 2)
    y_prompt = modulate(rmsnorm(xp, norm_final_g), fp[0], fp[1])
    y_sample = modulate(rmsnorm(xs, norm_final_g), fs[0], fs[1])
    p_sb_k = jnp.stack(p_rows[0])
    p_sb_v = jnp.stack(p_rows[1])
    p_diff_k = jnp.stack(p_rows[2])
    p_diff_v = jnp.stack(p_rows[3])
    s_sb_k = jnp.stack(s_rows[0])
    s_sb_v = jnp.stack(s_rows[1])
    s_diff_k = jnp.stack(s_rows[2])
    s_diff_v = jnp.stack(s_rows[3])
    return (y_prompt, y_sample, p_sb_k, p_sb_v, p_diff_k, p_diff_v, s_sb_k, s_sb_v, s_diff_k, s_diff_v)
```

```python
import functools
import math

import jax
import jax.numpy as jnp
import numpy as np
from jax import lax
from jax.experimental import pallas as pl
from jax.experimental.pallas import tpu as pltpu

F32 = jnp.float32
BF16 = jnp.bfloat16

EPS = 1e-6
ROPE_THETA = 500000.0
PEER_N_KEYS = 128
PEER_TOPK = 16
PEER_HEADS = 8
LANES = 128
VMEM_LIMIT = 56 * 1024 * 1024
TOKEN_BLOCK = 256
ATTN_BLOCK = 512
SB_CHUNK = 256

_NT = (((1,), (1,)), ((), ()))


def _params(*sem):
    return pltpu.CompilerParams(dimension_semantics=sem, vmem_limit_bytes=VMEM_LIMIT)


def _split_bf16(x):
    hi = x.astype(BF16)
    lo = (x - hi.astype(F32)).astype(BF16)
    return hi, lo


def _norm_mod(x, g, shift, scale):
    ms = jnp.mean(x * x, axis=-1, keepdims=True)
    y = x * lax.rsqrt(ms + EPS) * g
    return y * (1.0 + scale) + shift


def _adaln_kernel(c_ref, w_ref, b_ref, o_ref):
    c = c_ref[...]
    a = (c * jax.nn.sigmoid(c)).astype(BF16)
    o_ref[...] = jnp.dot(a, w_ref[...].astype(BF16), preferred_element_type=F32) + b_ref[...]


def _adaln(c, w, b, tn=1024):
    m, d = c.shape
    n = w.shape[1]
    return pl.pallas_call(
        _adaln_kernel,
        grid=(n // tn,),
        in_specs=[pl.BlockSpec((m, d), lambda j: (0, 0)),
                  pl.BlockSpec((d, tn), lambda j: (0, j)),
                  pl.BlockSpec((1, tn), lambda j: (0, j))],
        out_specs=pl.BlockSpec((m, tn), lambda j: (0, j)),
        out_shape=jax.ShapeDtypeStruct((m, n), F32),
        compiler_params=_params("arbitrary"),
        name="adaln",
    )(c, w, b.reshape(1, n))


def _rope_tables(pos):
    rot = 16
    lane = np.arange(LANES)
    sub = lane % 64
    inv = ROPE_THETA ** (-jnp.arange(0, rot, 2, dtype=F32) / rot)
    ang = pos.astype(F32)[:, None] * inv
    ang = ang[:, sub % 8]
    cos, sin = jnp.cos(ang), jnp.sin(ang)
    first = jnp.asarray(sub < 8)
    second = jnp.asarray((sub >= 8) & (sub < 16))
    c = jnp.where(first | second, cos, 1.0)
    s_lo = jnp.where(first, -sin, 0.0)
    s_hi = jnp.where(second, sin, 0.0)
    return c, s_lo, s_hi


def _inproj_kernel(x_ref, g_ref, sh_ref, sc_ref, w_ref, cos_ref, slo_ref, shi_ref,
                   o0, o1, o2, o3, o4, o5, h_scr):
    n = pl.program_id(1)

    @pl.when(n == 0)
    def _():
        h_scr[...] = _norm_mod(x_ref[...], g_ref[...], sh_ref[...], sc_ref[...]).astype(BF16)

    z = jnp.dot(h_scr[...], w_ref[...], preferred_element_type=F32)
    for idx, o in enumerate((o0, o1, o2, o3, o4, o5)):
        @pl.when(n == idx)
        def _(o=o, idx=idx):
            if idx in (3, 4):
                cos, slo, shi = cos_ref[...], slo_ref[...], shi_ref[...]
                for j in range(z.shape[1] // LANES):
                    zc = z[:, j * LANES:(j + 1) * LANES]
                    o[:, j * LANES:(j + 1) * LANES] = (
                        zc * cos + pltpu.roll(zc, LANES - 8, 1) * slo + pltpu.roll(zc, 8, 1) * shi)
            else:
                o[...] = z


def _inproj(x, g, mods, w_bf, pos, tb, blocks_per_batch):
    t, d = x.shape
    nb, r, _ = mods.shape
    wn = 1024
    nblk = w_bf.shape[1] // wn
    assert nblk == 6
    cos, slo, shi = _rope_tables(pos)
    npos = pos.shape[0] // tb
    bmap = lambda i: i // blocks_per_batch
    mod_spec = lambda k: pl.BlockSpec((None, r, d), lambda i, n: (bmap(i), 0, k))
    tab_spec = pl.BlockSpec((tb, LANES), lambda i, n: (i % npos, 0))
    out_spec = pl.BlockSpec((tb, wn), lambda i, n: (i, 0))
    return pl.pallas_call(
        _inproj_kernel,
        grid=(t // tb, nblk),
        in_specs=[pl.BlockSpec((tb, d), lambda i, n: (i, 0)),
                  pl.BlockSpec((1, d), lambda i, n: (0, 0)),
                  mod_spec(0), mod_spec(1),
                  pl.BlockSpec((d, wn), lambda i, n: (0, n)),
                  tab_spec, tab_spec, tab_spec],
        out_specs=[out_spec] * 6,
        out_shape=[jax.ShapeDtypeStruct((t, wn), F32)] * 6,
        scratch_shapes=[pltpu.VMEM((tb, d), BF16)],
        compiler_params=_params("arbitrary", "arbitrary"),
        name="inproj",
    )(x, g.reshape(1, d), mods, mods, w_bf, cos, slo, shi)


def _softplus(z):
    return jnp.maximum(z, 0.0) + jnp.log1p(jnp.exp(-jnp.abs(z)))


def _suffix_matrix(n):
    s = lax.broadcasted_iota(jnp.int32, (n, n), 0)
    j = lax.broadcasted_iota(jnp.int32, (n, n), 1)
    return jnp.where(s > j, 1.0, 0.0).astype(BF16)


def _sb_chunk(z, v_bf, c, mask):
    sp = _softplus(z)
    lk = -sp if mask is None else jnp.where(mask, -sp, 0.0)
    lk_hi, lk_lo = _split_bf16(lk)
    u = _suffix_matrix(z.shape[1])
    after = (jnp.dot(lk_hi, u, preferred_element_type=F32)
             + jnp.dot(lk_lo, u, preferred_element_type=F32))
    a = jnp.exp((z - sp) + (after + c))
    if mask is not None:
        a = jnp.where(mask, a, 0.0)
    contrib = jnp.dot(a.astype(BF16), v_bf, preferred_element_type=F32)
    return contrib, c + after[:, 0:1] + lk[:, 0:1]


def _sb_prompt_kernel(qb_ref, kb_ref, q_ref, k_ref, v_ref, o_ref, acc_ref, c_ref, *, tq, sub, scale):
    p = pl.program_id(2)
    qb, kb = qb_ref[p], kb_ref[p]

    @pl.when(kb == qb)
    def _():
        acc_ref[...] = jnp.zeros_like(acc_ref)
        c_ref[...] = jnp.zeros_like(c_ref)

    q = q_ref[...].astype(BF16)
    off = (qb - kb) * tq
    row = lax.broadcasted_iota(jnp.int32, (tq, sub), 0)
    col = lax.broadcasted_iota(jnp.int32, (tq, sub), 1)
    acc = acc_ref[...]
    c = c_ref[...]
    for s in reversed(range(tq // sub)):
        k = k_ref[s * sub:(s + 1) * sub, :].astype(BF16)
        v = v_ref[s * sub:(s + 1) * sub, :].astype(BF16)
        z = lax.dot_general(q, k, _NT, preferred_element_type=F32) * scale
        mask = (col + s * sub) < (row + off)
        contrib, c = _sb_chunk(z, v, c, mask)
        acc = acc + contrib
    acc_ref[...] = acc
    c_ref[...] = c

    @pl.when(kb == 0)
    def _():
        o_ref[...] = acc


def _tri_pairs(nq):
    qs, ks = [], []
    for q in range(nq):
        for k in range(q, -1, -1):
            qs.append(q)
            ks.append(k)
    return jnp.asarray(qs, jnp.int32), jnp.asarray(ks, jnp.int32)


def _sb_prompt(q, k, v, batch, heads, tq, sub):
    t, w = q.shape
    hd = w // heads
    nq = t // batch // tq
    qs, ks = _tri_pairs(nq)
    qspec = pl.BlockSpec((tq, hd), lambda b, h, p, qa, ka: (b * nq + qa[p], h))
    kspec = pl.BlockSpec((tq, hd), lambda b, h, p, qa, ka: (b * nq + ka[p], h))
    return pl.pallas_call(
        functools.partial(_sb_prompt_kernel, tq=tq, sub=sub, scale=1.0 / math.sqrt(hd)),
        grid_spec=pltpu.PrefetchScalarGridSpec(
            num_scalar_prefetch=2,
            grid=(batch, heads, qs.shape[0]),
            in_specs=[qspec, kspec, kspec],
            out_specs=qspec,
            scratch_shapes=[pltpu.VMEM((tq, hd), F32), pltpu.VMEM((tq, 1), F32)]),
        out_shape=jax.ShapeDtypeStruct((t, w), F32),
        compiler_params=_params("arbitrary", "arbitrary", "arbitrary"),
        name="sb_prompt",
    )(qs, ks, q, k, v)


def _lambda(lp, lam_init):
    a = jnp.sum(lp[0:1, :] * lp[1:2, :], axis=1, keepdims=True)
    b = jnp.sum(lp[2:3, :] * lp[3:4, :], axis=1, keepdims=True)
    return jnp.exp(a) - jnp.exp(b) + lam_init


def _diff_prompt_kernel(qb_ref, kb_ref, q_ref, k_ref, v_ref, lam_ref, o_ref,
                        acc_ref, m_ref, l_ref, *, tq, dqk, scale, lam_init):
    p = pl.program_id(2)
    qb, kb = qb_ref[p], kb_ref[p]

    @pl.when(kb == qb)
    def _():
        acc_ref[...] = jnp.zeros_like(acc_ref)
        m_ref[...] = jnp.full_like(m_ref, -jnp.inf)
        l_ref[...] = jnp.zeros_like(l_ref)

    q = q_ref[...].astype(BF16)
    k = k_ref[...].astype(BF16)
    v = v_ref[...].astype(BF16)
    off = (qb - kb) * tq
    row = lax.broadcasted_iota(jnp.int32, (tq, tq), 0)
    col = lax.broadcasted_iota(jnp.int32, (tq, tq), 1)
    mask = col <= (row + off)
    for c in range(2):
        s = lax.dot_general(q[:, c * dqk:(c + 1) * dqk], k[:, c * dqk:(c + 1) * dqk], _NT,
                            preferred_element_type=F32) * scale
        s = jnp.where(mask, s, -jnp.inf)
        m_old = m_ref[c]
        m_new = jnp.maximum(m_old, jnp.max(s, axis=1, keepdims=True))
        alpha = jnp.exp(m_old - m_new)
        pr = jnp.exp(s - m_new)
        l_ref[c] = alpha * l_ref[c] + jnp.sum(pr, axis=1, keepdims=True)
        acc_ref[c] = alpha * acc_ref[c] + jnp.dot(pr.astype(BF16), v, preferred_element_type=F32)
        m_ref[c] = m_new

    @pl.when(kb == 0)
    def _():
        lam = _lambda(lam_ref[...], lam_init)
        o_ref[...] = acc_ref[0] / l_ref[0] - lam * (acc_ref[1] / l_ref[1])


def _diff_prompt(q, k, v, lam_p, lam_init, batch, heads, tq):
    t, w = q.shape
    hd = w // heads
    dqk = hd // 2
    nq = t // batch // tq
    qs, ks = _tri_pairs(nq)
    qspec = pl.BlockSpec((tq, hd), lambda b, h, p, qa, ka: (b * nq + qa[p], h))
    kspec = pl.BlockSpec((tq, hd), lambda b, h, p, qa, ka: (b * nq + ka[p], h))
    return pl.pallas_call(
        functools.partial(_diff_prompt_kernel, tq=tq, dqk=dqk, scale=1.0 / math.sqrt(dqk),
                          lam_init=lam_init),
        grid_spec=pltpu.PrefetchScalarGridSpec(
            num_scalar_prefetch=2,
            grid=(batch, heads, qs.shape[0]),
            in_specs=[qspec, kspec, kspec,
                      pl.BlockSpec(lam_p.shape, lambda b, h, p, qa, ka: (0, 0))],
            out_specs=qspec,
            scratch_shapes=[pltpu.VMEM((2, tq, hd), F32), pltpu.VMEM((2, tq, 1), F32),
                            pltpu.VMEM((2, tq, 1), F32)]),
        out_shape=jax.ShapeDtypeStruct((t, w), F32),
        compiler_params=_params("arbitrary", "arbitrary", "arbitrary"),
        name="diff_prompt",
    )(qs, ks, q, k, v, lam_p)


def _sample_attn_kernel(pt_ref, sbq_ref, dq_ref, dkn_ref, dvn_ref, ksb_ref, vsb_ref, kd_ref, vd_ref,
                        lam_ref, osb_ref, od_ref,
                        qsb_scr, qd_scr, acc_sb, c_scr, acc_d, m_scr, l_scr,
                        *, heads, hd, lam_init):
    j = pl.program_id(1)
    w = heads * hd
    dqk = hd // 2
    rows = 2 * heads
    r_i = lax.broadcasted_iota(jnp.int32, (rows, w), 0)
    l_i = lax.broadcasted_iota(jnp.int32, (rows, w), 1)
    sb_mask = r_i == l_i // hd
    d_mask = (r_i % heads == l_i // hd) & (r_i // heads == (l_i // dqk) % 2)

    @pl.when(j == 0)
    def _():
        qsb = jnp.where(sb_mask, sbq_ref[...], 0.0).astype(BF16)
        qd = jnp.where(d_mask, dq_ref[...], 0.0).astype(BF16)
        qsb_scr[...] = qsb
        qd_scr[...] = qd
        acc_sb[...] = jnp.zeros_like(acc_sb)
        c_scr[...] = jnp.zeros_like(c_scr)
        kn = dkn_ref[...].astype(BF16).astype(F32)
        vn = dvn_ref[...].astype(BF16).astype(F32)
        m_scr[...] = jnp.sum(qd.astype(F32) * kn, axis=1, keepdims=True) * (1.0 / math.sqrt(dqk))
        l_scr[...] = jnp.ones_like(l_scr)
        acc_d[...] = jnp.broadcast_to(vn, (rows, w))

    z = lax.dot_general(qsb_scr[...], ksb_ref[...].astype(BF16), _NT,
                        preferred_element_type=F32) * (1.0 / math.sqrt(hd))
    contrib, c_new = _sb_chunk(z, vsb_ref[...].astype(BF16), c_scr[...], None)
    acc_sb[...] += contrib
    c_scr[...] = c_new

    s = lax.dot_general(qd_scr[...], kd_ref[...].astype(BF16), _NT,
                        preferred_element_type=F32) * (1.0 / math.sqrt(dqk))
    m_old = m_scr[...]
    m_new = jnp.maximum(m_old, jnp.max(s, axis=1, keepdims=True))
    alpha = jnp.exp(m_old - m_new)
    pr = jnp.exp(s - m_new)
    l_scr[...] = alpha * l_scr[...] + jnp.sum(pr, axis=1, keepdims=True)
    acc_d[...] = alpha * acc_d[...] + jnp.dot(pr.astype(BF16), vd_ref[...].astype(BF16),
                                               preferred_element_type=F32)
    m_scr[...] = m_new

    @pl.when(j == pl.num_programs(1) - 1)
    def _():
        osb_ref[...] = jnp.sum(jnp.where(sb_mask, acc_sb[...], 0.0), axis=0, keepdims=True)
        lam = _lambda(lam_ref[...], lam_init)
        nrm = acc_d[...] / l_scr[...]
        sign = jnp.where(r_i < heads, 1.0, -lam)
        own_head = r_i % heads == l_i // hd
        od_ref[...] = jnp.sum(jnp.where(own_head, nrm * sign, 0.0), axis=0, keepdims=True)


def _sample_attn(sbq, dq, dkn, dvn, c_sbk, c_sbv, c_dk, c_dv, page_table, lam_p, lam_init, heads):
    n, w = sbq.shape
    hd = w // heads
    npages = page_table.shape[1]
    page = c_sbk.shape[1]
    rows = 2 * heads
    row_spec = pl.BlockSpec((None, 1, w), lambda b, j, pt: (b, 0, 0))
    cache_spec = pl.BlockSpec((None, page, w), lambda b, j, pt: (pt[b, npages - 1 - j], 0, 0))
    r3 = lambda a: a.reshape(n, 1, w)
    osb, od = pl.pallas_call(
        functools.partial(_sample_attn_kernel, heads=heads, hd=hd, lam_init=lam_init),
        grid_spec=pltpu.PrefetchScalarGridSpec(
            num_scalar_prefetch=1,
            grid=(n, npages),
            in_specs=[row_spec] * 4 + [cache_spec] * 4
                     + [pl.BlockSpec(lam_p.shape, lambda b, j, pt: (0, 0))],
            out_specs=[row_spec, row_spec],
            scratch_shapes=[pltpu.VMEM((rows, w), BF16), pltpu.VMEM((rows, w), BF16),
                            pltpu.VMEM((rows, w), F32), pltpu.VMEM((rows, 1), F32),
                            pltpu.VMEM((rows, w), F32), pltpu.VMEM((rows, 1), F32),
                            pltpu.VMEM((rows, 1), F32)]),
        out_shape=[jax.ShapeDtypeStruct((n, 1, w), F32)] * 2,
        compiler_params=_params("arbitrary", "arbitrary"),
        name="sample_attn",
    )(page_table, r3(sbq), r3(dq), r3(dkn), r3(dvn), c_sbk, c_sbv, c_dk, c_dv, lam_p)
    return osb.reshape(n, w), od.reshape(n, w)


def _merge_kernel(sbo_ref, do_ref, x_ref, w_ref, dg_ref, gate_ref, g2_ref, sh_ref, sc_ref,
                  x2_ref, h2_ref, o_scr, *, heads, hd, lam_init):
    sbw = heads * hd
    o_scr[:, :sbw] = sbo_ref[...].astype(BF16)
    dg = dg_ref[...]
    for h in range(heads):
        d = do_ref[:, h * hd:(h + 1) * hd]
        ms = jnp.mean(d * d, axis=-1, keepdims=True)
        dn = (d * lax.rsqrt(ms + EPS) * dg) * (1.0 - lam_init)
        o_scr[:, sbw + h * hd:sbw + (h + 1) * hd] = dn.astype(BF16)
    o = jnp.dot(o_scr[...], w_ref[...], preferred_element_type=F32)
    x2 = x_ref[...] + gate_ref[...] * o
    x2_ref[...] = x2
    h2_ref[...] = _norm_mod(x2, g2_ref[...], sh_ref[...], sc_ref[...])


def _merge(sbo, do, x, w_bf, dg, g2, mods, lam_init, heads, tb, blocks_per_batch):
    t, d = x.shape
    nb, r, _ = mods.shape
    w = sbo.shape[1]
    hd = w // heads
    bmap = lambda i: i // blocks_per_batch
    mod_spec = lambda k: pl.BlockSpec((None, r, d), lambda i: (bmap(i), 0, k))
    tok = lambda width: pl.BlockSpec((tb, width), lambda i: (i, 0))
    return pl.pallas_call(
        functools.partial(_merge_kernel, heads=heads, hd=hd, lam_init=lam_init),
        grid=(t // tb,),
        in_specs=[tok(w), tok(w), tok(d),
                  pl.BlockSpec(w_bf.shape, lambda i: (0, 0)),
                  pl.BlockSpec((1, hd), lambda i: (0, 0)),
                  mod_spec(2),
                  pl.BlockSpec((1, d), lambda i: (0, 0)),
                  mod_spec(3), mod_spec(4)],
        out_specs=[tok(d), tok(d)],
        out_shape=[jax.ShapeDtypeStruct((t, d), F32)] * 2,
        scratch_shapes=[pltpu.VMEM((tb, 2 * w), BF16)],
        compiler_params=_params("arbitrary"),
        name="merge_outproj",
    )(sbo, do, x, w_bf, dg.reshape(1, hd), mods, g2.reshape(1, d), mods, mods)


def _topk_rows(s, k):
    n = s.shape[0]
    idx = lax.broadcasted_iota(jnp.int32, s.shape, 0).astype(F32)
    vals, ids = [], []
    for _ in range(k):
        m = jnp.max(s, axis=0, keepdims=True)
        i = jnp.min(jnp.where(s == m, idx, float(n)), axis=0, keepdims=True)
        s = jnp.where(idx == i, -jnp.inf, s)
        vals.append(m)
        ids.append(i)
    return jnp.concatenate(vals, axis=0), jnp.concatenate(ids, axis=0)


def _route_kernel(h_ref, whi_ref, wlo_ref, khi_ref, klo_ref, i_ref, j_ref, g_ref, *, half, topk):
    h_hi, h_lo = _split_bf16(h_ref[...])
    dot = functools.partial(jnp.dot, preferred_element_type=F32)
    q = dot(h_hi, whi_ref[...]) + dot(h_hi, wlo_ref[...]) + dot(h_lo, whi_ref[...])
    tops = []
    for c in range(2):
        q_hi, q_lo = _split_bf16(q[:, c * half:(c + 1) * half])
        k_hi, k_lo = khi_ref[c], klo_ref[c]
        nt = functools.partial(lax.dot_general, dimension_numbers=_NT, preferred_element_type=F32)
        s = nt(k_hi, q_hi) + nt(k_hi, q_lo) + nt(k_lo, q_hi)
        tops.append(_topk_rows(s, topk))
    (v0, i0), (v1, i1) = tops
    tb = v0.shape[1]
    cand = jnp.stack([v0[a:a + 1, :] + v1 for a in range(topk)], axis=0)
    pos = (lax.broadcasted_iota(jnp.int32, (topk, topk, tb), 0) * topk
           + lax.broadcasted_iota(jnp.int32, (topk, topk, tb), 1)).astype(F32)
    a_iota = lax.broadcasted_iota(jnp.int32, (topk, tb), 0).astype(F32)
    best, sel_i, sel_j = [], [], []
    for _ in range(topk):
        m = jnp.max(jnp.max(cand, axis=0), axis=0, keepdims=True)
        p = jnp.min(jnp.min(jnp.where(cand == m[None], pos, float(topk * topk)), axis=0),
                    axis=0, keepdims=True)
        cand = jnp.where(pos == p[None], -jnp.inf, cand)
        pa = jnp.floor(p * (1.0 / topk))
        pb = p - pa * topk
        best.append(m)
        sel_i.append(jnp.sum(jnp.where(a_iota == pa, i0, 0.0), axis=0, keepdims=True))
        sel_j.append(jnp.sum(jnp.where(a_iota == pb, i1, 0.0), axis=0, keepdims=True))
    best = jnp.concatenate(best, axis=0)
    e = jnp.exp(best - best[0:1, :])
    g_ref[...] = e / jnp.sum(e, axis=0, keepdims=True)
    i_ref[...] = jnp.concatenate(sel_i, axis=0).astype(jnp.int32)
    j_ref[...] = jnp.concatenate(sel_j, axis=0).astype(jnp.int32)


def _route(h, wq_hi, wq_lo, keys_hi, keys_lo, tb):
    t, d = h.shape
    heads, _, n_keys, half = keys_hi.shape
    topk = PEER_TOPK
    out_spec = pl.BlockSpec((topk, tb), lambda i, hh: (hh, i))
    key_spec = pl.BlockSpec((None, 2, n_keys, half), lambda i, hh: (hh, 0, 0, 0))
    w_spec = pl.BlockSpec((d, 2 * half), lambda i, hh: (0, hh))
    return pl.pallas_call(
        functools.partial(_route_kernel, half=half, topk=topk),
        grid=(t // tb, heads),
        in_specs=[pl.BlockSpec((tb, d), lambda i, hh: (i, 0)), w_spec, w_spec, key_spec, key_spec],
        out_specs=[out_spec] * 3,
        out_shape=[jax.ShapeDtypeStruct((heads * topk, t), jnp.int32)] * 2
                  + [jax.ShapeDtypeStruct((heads * topk, t), F32)],
        compiler_params=_params("arbitrary", "arbitrary"),
        name="peer_route",
    )(h, wq_hi, wq_lo, keys_hi, keys_lo)


def _gelu(x):
    return 0.5 * x * (1.0 + lax.erf(x * (1.0 / math.sqrt(2.0))))


def _peer_kernel(h_ref, i_ref, j_ref, gt_ref, u_ref, v_ref, x_ref, gate_ref, gf_ref, sh_ref, sc_ref,
                 y_ref, gmat, hb_scr, acc_ref, *, tb, n_keys, eb):
    e = pl.program_id(1)
    per = eb // n_keys

    @pl.when(e == 0)
    def _():
        hb_scr[...] = h_ref[...].astype(BF16)
        acc_ref[...] = jnp.zeros_like(acc_ref)
        sub = lax.broadcasted_iota(jnp.int32, (n_keys, n_keys), 0)

        def build(t, carry):
            irow = i_ref[pl.ds(t, 1), :]
            jrow = j_ref[pl.ds(t, 1), :]
            g = gt_ref[pl.ds(t, 1), :]
            g_hi = g.astype(BF16).astype(F32)
            hit_i = sub == irow
            a_hi = jnp.where(hit_i, g_hi, 0.0).astype(BF16)
            a_lo = jnp.where(hit_i, g - g_hi, 0.0).astype(BF16)
            b = jnp.where(sub == jrow, 1.0, 0.0).astype(BF16)
            gt = (lax.dot_general(a_hi, b, _NT, preferred_element_type=F32)
                  + lax.dot_general(a_lo, b, _NT, preferred_element_type=F32))
            gmat[pl.ds(pl.multiple_of(t * n_keys, n_keys), n_keys), :] = gt
            return carry

        lax.fori_loop(0, tb, build, 0)

    s = lax.dot_general(hb_scr[...], u_ref[...], _NT, preferred_element_type=F32)
    gate = jnp.concatenate(
        [gmat[pl.ds(e * per + r, tb, stride=n_keys), :] for r in range(per)], axis=1)
    wd = (gate * _gelu(s)).astype(BF16)
    acc_ref[...] += jnp.dot(wd, v_ref[...], preferred_element_type=F32)

    @pl.when(e == pl.num_programs(1) - 1)
    def _():
        x3 = x_ref[...] + gate_ref[...] * acc_ref[...]
        y_ref[...] = _norm_mod(x3, gf_ref[...], sh_ref[...], sc_ref[...])


def _peer(h, i_sel, j_sel, g_sel, u_bf, v_bf, x, mods, fmods, gf, tb, blocks_per_batch, eb=256):
    t, d = h.shape
    n_exp = u_bf.shape[0]
    n_keys = PEER_N_KEYS
    npick = i_sel.shape[1]
    nb, r, _ = mods.shape
    bmap = lambda i: i // blocks_per_batch
    tok = lambda width: pl.BlockSpec((tb, width), lambda i, e: (i, 0))
    tab = pl.BlockSpec((eb, d), lambda i, e: (e, 0))
    return pl.pallas_call(
        functools.partial(_peer_kernel, tb=tb, n_keys=n_keys, eb=eb),
        grid=(t // tb, n_exp // eb),
        in_specs=[tok(d), tok(npick), tok(npick), tok(npick), tab, tab, tok(d),
                  pl.BlockSpec((None, r, d), lambda i, e: (bmap(i), 0, 5)),
                  pl.BlockSpec((1, d), lambda i, e: (0, 0)),
                  pl.BlockSpec((None, r, d), lambda i, e: (bmap(i), 0, 0)),
                  pl.BlockSpec((None, r, d), lambda i, e: (bmap(i), 0, 1))],
        out_specs=tok(d),
        out_shape=jax.ShapeDtypeStruct((t, d), F32),
        scratch_shapes=[pltpu.VMEM((tb * n_keys, n_keys), F32), pltpu.VMEM((tb, d), BF16),
                        pltpu.VMEM((tb, d), F32)],
        compiler_params=_params("arbitrary", "arbitrary"),
        name="peer_mix",
    )(h, i_sel, j_sel, g_sel, u_bf, v_bf, x, mods, gf.reshape(1, d), fmods, fmods)


def _token_path(x, mods, fmods, pos, tb, blocks_per_batch, attend, w):
    proj = _inproj(x, w["norm_mix_g"], mods, w["w_in"], pos, tb, blocks_per_batch)
    sbo, do = attend(*proj)
    x2, h2 = _merge(sbo, do, x, w["w_out"], w["diff_norm_g"], w["norm_ffn_g"], mods,
                    w["lam_init"], w["heads"], tb, blocks_per_batch)
    i_sel, j_sel, g_sel = _route(h2, w["wq_hi"], w["wq_lo"], w["keys_hi"], w["keys_lo"], tb)
    y = _peer(h2, i_sel.T, j_sel.T, g_sel.T, w["u"], w["v"], x2, mods, fmods,
              w["norm_final_g"], tb, blocks_per_batch)
    return y, proj


def kernel(x_prompt, x_sample, c_prompt, c_sample, cache_sb_k, cache_sb_v, cache_diff_k, cache_diff_v, page_table, norm_mix_g, norm_ffn_g, w_ada, b_ada, w_in, diff_lambda, diff_norm_g, w_out, peer_w_q, peer_sub_keys, peer_u, peer_v, norm_final_g, w_ada_final, b_ada_final):
    batch, seq, d = x_prompt.shape
    nsamp, dec_seq, _ = x_sample.shape
    depth = w_ada.shape[0]
    assert depth == 1 and dec_seq == 1
    heads = cache_sb_k.shape[3]
    hd = cache_sb_k.shape[4]
    past_len = page_table.shape[1] * cache_sb_k.shape[2]
    l = 0
    lam_init = 0.8 - 0.6 * math.exp(-0.3 * l)

    nc = batch + nsamp
    c_all = jnp.concatenate([c_prompt, c_sample, jnp.zeros((-nc % 8, d), F32)], axis=0)
    mods = _adaln(c_all, w_ada[l], b_ada[l])
    fmods = _adaln(c_all, w_ada_final, b_ada_final)
    mods_p, mods_s = mods[:batch, None, :], mods[None, batch:nc, :]
    fmods_p, fmods_s = fmods[:batch, None, :], fmods[None, batch:nc, :]

    wq_hi, wq_lo = _split_bf16(peer_w_q[l])
    keys_hi, keys_lo = _split_bf16(peer_sub_keys[l])
    w = dict(norm_mix_g=norm_mix_g[l], norm_ffn_g=norm_ffn_g[l], w_in=w_in[l].astype(BF16),
             w_out=w_out[l].astype(BF16), diff_norm_g=diff_norm_g[l], lam_init=lam_init, heads=heads,
             wq_hi=wq_hi, wq_lo=wq_lo, keys_hi=keys_hi, keys_lo=keys_lo,
             u=peer_u[l].astype(BF16), v=peer_v[l].astype(BF16), norm_final_g=norm_final_g)
    lam_p = diff_lambda[l]

    def attend_prompt(sb_q, sb_k, sb_v, d_q, d_k, d_v):
        return (_sb_prompt(sb_q, sb_k, sb_v, batch, heads, ATTN_BLOCK, SB_CHUNK),
                _diff_prompt(d_q, d_k, d_v, lam_p, lam_init, batch, heads, ATTN_BLOCK))

    def attend_sample(sb_q, sb_k, sb_v, d_q, d_k, d_v):
        flat = lambda c: c.reshape(c.shape[1], c.shape[2], heads * hd)
        return _sample_attn(sb_q, d_q, d_k, d_v, flat(cache_sb_k), flat(cache_sb_v),
                            flat(cache_diff_k), flat(cache_diff_v), page_table, lam_p, lam_init, heads)

    tb_p = TOKEN_BLOCK
    y_p, proj_p = _token_path(x_prompt.reshape(batch * seq, d), mods_p, fmods_p, jnp.arange(seq),
                              tb_p, seq // tb_p, attend_prompt, w)
    y_s, proj_s = _token_path(x_sample.reshape(nsamp, d), mods_s, fmods_s,
                              jnp.full((nsamp,), past_len, jnp.int32), nsamp, 1, attend_sample, w)

    def rows(proj, n, t):
        _, sb_k, sb_v, _, d_k, d_v = proj
        return (sb_k.reshape(1, n, t, heads, hd), sb_v.reshape(1, n, t, heads, hd),
                d_k.reshape(1, n, t, heads, 2, hd // 2), d_v.reshape(1, n, t, heads, hd))

    return ((y_p.reshape(batch, seq, d), y_s.reshape(nsamp, dec_seq, d))
            + rows(proj_p, batch, seq) + rows(proj_s, nsamp, dec_seq))
```

```python
import functools
import math

import jax
import jax.numpy as jnp
import numpy as np
from jax import lax
from jax.experimental import pallas as pl
from jax.experimental.pallas import tpu as pltpu

F32 = jnp.float32
BF16 = jnp.bfloat16

EPS = 1e-6
ROPE_THETA = 500000.0
PEER_N_KEYS = 128
PEER_TOPK = 16
PEER_HEADS = 8
LANES = 128
VMEM_LIMIT = 56 * 1024 * 1024
TOKEN_BLOCK = 256
ATTN_BLOCK = 512
SB_CHUNK = 256
PAGES_PER_STEP = 4
PEER_EXPERT_BLOCK = 512
PEER_CHUNK = 256
PEER_BUILD_UNROLL = 8

_NT = (((1,), (1,)), ((), ()))


def _params(*sem):
    return pltpu.CompilerParams(dimension_semantics=sem, vmem_limit_bytes=VMEM_LIMIT)


def _split_bf16(x):
    hi = x.astype(BF16)
    lo = (x - hi.astype(F32)).astype(BF16)
    return hi, lo


def _norm_mod(x, g, shift, scale):
    ms = jnp.mean(x * x, axis=-1, keepdims=True)
    y = x * lax.rsqrt(ms + EPS) * g
    return y * (1.0 + scale) + shift


def _adaln_kernel(c_ref, w_ref, b_ref, o_ref):
    c = c_ref[...]
    a = (c * jax.nn.sigmoid(c)).astype(BF16)
    o_ref[...] = jnp.dot(a, w_ref[...].astype(BF16), preferred_element_type=F32) + b_ref[...]


def _adaln(c, w, b, tn=1024):
    m, d = c.shape
    n = w.shape[1]
    return pl.pallas_call(
        _adaln_kernel,
        grid=(n // tn,),
        in_specs=[pl.BlockSpec((m, d), lambda j: (0, 0)),
                  pl.BlockSpec((d, tn), lambda j: (0, j)),
                  pl.BlockSpec((1, tn), lambda j: (0, j))],
        out_specs=pl.BlockSpec((m, tn), lambda j: (0, j)),
        out_shape=jax.ShapeDtypeStruct((m, n), F32),
        compiler_params=_params("arbitrary"),
        name="adaln",
    )(c, w, b.reshape(1, n))


def _rope_tables(pos):
    rot = 16
    lane = np.arange(LANES)
    sub = lane % 64
    inv = ROPE_THETA ** (-jnp.arange(0, rot, 2, dtype=F32) / rot)
    ang = pos.astype(F32)[:, None] * inv
    ang = ang[:, sub % 8]
    cos, sin = jnp.cos(ang), jnp.sin(ang)
    first = jnp.asarray(sub < 8)
    second = jnp.asarray((sub >= 8) & (sub < 16))
    c = jnp.where(first | second, cos, 1.0)
    s_lo = jnp.where(first, -sin, 0.0)
    s_hi = jnp.where(second, sin, 0.0)
    return c, s_lo, s_hi


def _inproj_kernel(x_ref, g_ref, sh_ref, sc_ref, w_ref, cos_ref, slo_ref, shi_ref,
                   o0, o1, o2, o3, o4, o5, h_scr):
    n = pl.program_id(1)

    @pl.when(n == 0)
    def _():
        h_scr[...] = _norm_mod(x_ref[...], g_ref[...], sh_ref[...], sc_ref[...]).astype(BF16)

    z = jnp.dot(h_scr[...], w_ref[...], preferred_element_type=F32)
    for idx, o in enumerate((o0, o1, o2, o3, o4, o5)):
        @pl.when(n == idx)
        def _(o=o, idx=idx):
            if idx in (3, 4):
                cos, slo, shi = cos_ref[...], slo_ref[...], shi_ref[...]
                for j in range(z.shape[1] // LANES):
                    zc = z[:, j * LANES:(j + 1) * LANES]
                    o[:, j * LANES:(j + 1) * LANES] = (
                        zc * cos + pltpu.roll(zc, LANES - 8, 1) * slo + pltpu.roll(zc, 8, 1) * shi)
            else:
                o[...] = z


def _inproj(x, g, mods, w_bf, pos, tb, blocks_per_batch):
    t, d = x.shape
    nb, r, _ = mods.shape
    wn = 1024
    nblk = w_bf.shape[1] // wn
    assert nblk == 6
    cos, slo, shi = _rope_tables(pos)
    npos = pos.shape[0] // tb
    bmap = lambda i: i // blocks_per_batch
    mod_spec = lambda k: pl.BlockSpec((None, r, d), lambda i, n: (bmap(i), 0, k))
    tab_spec = pl.BlockSpec((tb, LANES), lambda i, n: (i % npos, 0))
    out_spec = pl.BlockSpec((tb, wn), lambda i, n: (i, 0))
    return pl.pallas_call(
        _inproj_kernel,
        grid=(t // tb, nblk),
        in_specs=[pl.BlockSpec((tb, d), lambda i, n: (i, 0)),
                  pl.BlockSpec((1, d), lambda i, n: (0, 0)),
                  mod_spec(0), mod_spec(1),
                  pl.BlockSpec((d, wn), lambda i, n: (0, n)),
                  tab_spec, tab_spec, tab_spec],
        out_specs=[out_spec] * 6,
        out_shape=[jax.ShapeDtypeStruct((t, wn), F32)] * 6,
        scratch_shapes=[pltpu.VMEM((tb, d), BF16)],
        compiler_params=_params("arbitrary", "arbitrary"),
        name="inproj",
    )(x, g.reshape(1, d), mods, mods, w_bf, cos, slo, shi)


def _softplus(z):
    return jnp.maximum(z, 0.0) + jnp.log1p(jnp.exp(-jnp.abs(z)))


def _suffix_matrix(n):
    s = lax.broadcasted_iota(jnp.int32, (n, n), 0)
    j = lax.broadcasted_iota(jnp.int32, (n, n), 1)
    return jnp.where(s > j, 1.0, 0.0).astype(BF16)


def _sb_chunk(z, v_bf, c, mask):
    sp = _softplus(z)
    lk = -sp if mask is None else jnp.where(mask, -sp, 0.0)
    lk_hi, lk_lo = _split_bf16(lk)
    u = _suffix_matrix(z.shape[1])
    after = (jnp.dot(lk_hi, u, preferred_element_type=F32)
             + jnp.dot(lk_lo, u, preferred_element_type=F32))
    a = jnp.exp((z - sp) + (after + c))
    if mask is not None:
        a = jnp.where(mask, a, 0.0)
    contrib = jnp.dot(a.astype(BF16), v_bf, preferred_element_type=F32)
    return contrib, c + after[:, 0:1] + lk[:, 0:1]


def _sb_prompt_kernel(qb_ref, kb_ref, q_ref, k_ref, v_ref, o_ref, acc_ref, c_ref, *, tq, sub, scale):
    p = pl.program_id(2)
    qb, kb = qb_ref[p], kb_ref[p]

    @pl.when(kb == qb)
    def _():
        acc_ref[...] = jnp.zeros_like(acc_ref)
        c_ref[...] = jnp.zeros_like(c_ref)

    q = q_ref[...].astype(BF16)
    off = (qb - kb) * tq
    row = lax.broadcasted_iota(jnp.int32, (tq, sub), 0)
    col = lax.broadcasted_iota(jnp.int32, (tq, sub), 1)
    acc = acc_ref[...]
    c = c_ref[...]
    for s in reversed(range(tq // sub)):
        k = k_ref[s * sub:(s + 1) * sub, :].astype(BF16)
        v = v_ref[s * sub:(s + 1) * sub, :].astype(BF16)
        z = lax.dot_general(q, k, _NT, preferred_element_type=F32) * scale
        mask = (col + s * sub) < (row + off)
        contrib, c = _sb_chunk(z, v, c, mask)
        acc = acc + contrib
    acc_ref[...] = acc
    c_ref[...] = c

    @pl.when(kb == 0)
    def _():
        o_ref[...] = acc


def _tri_pairs(nq):
    qs, ks = [], []
    for q in range(nq):
        for k in range(q, -1, -1):
            qs.append(q)
            ks.append(k)
    return jnp.asarray(qs, jnp.int32), jnp.asarray(ks, jnp.int32)


def _sb_prompt(q, k, v, batch, heads, tq, sub):
    t, w = q.shape
    hd = w // heads
    nq = t // batch // tq
    qs, ks = _tri_pairs(nq)
    qspec = pl.BlockSpec((tq, hd), lambda b, h, p, qa, ka: (b * nq + qa[p], h))
    kspec = pl.BlockSpec((tq, hd), lambda b, h, p, qa, ka: (b * nq + ka[p], h))
    return pl.pallas_call(
        functools.partial(_sb_prompt_kernel, tq=tq, sub=sub, scale=1.0 / math.sqrt(hd)),
        grid_spec=pltpu.PrefetchScalarGridSpec(
            num_scalar_prefetch=2,
            grid=(batch, heads, qs.shape[0]),
            in_specs=[qspec, kspec, kspec],
            out_specs=qspec,
            scratch_shapes=[pltpu.VMEM((tq, hd), F32), pltpu.VMEM((tq, 1), F32)]),
        out_shape=jax.ShapeDtypeStruct((t, w), F32),
        compiler_params=_params("arbitrary", "arbitrary", "arbitrary"),
        name="sb_prompt",
    )(qs, ks, q, k, v)


def _lambda(lp, lam_init):
    a = jnp.sum(lp[0:1, :] * lp[1:2, :], axis=1, keepdims=True)
    b = jnp.sum(lp[2:3, :] * lp[3:4, :], axis=1, keepdims=True)
    return jnp.exp(a) - jnp.exp(b) + lam_init


def _diff_prompt_kernel(qb_ref, kb_ref, q_ref, k_ref, v_ref, lam_ref, o_ref,
                        acc_ref, m_ref, l_ref, *, tq, dqk, scale, lam_init):
    p = pl.program_id(2)
    qb, kb = qb_ref[p], kb_ref[p]

    @pl.when(kb == qb)
    def _():
        acc_ref[...] = jnp.zeros_like(acc_ref)
        m_ref[...] = jnp.full_like(m_ref, -jnp.inf)
        l_ref[...] = jnp.zeros_like(l_ref)

    q = q_ref[...].astype(BF16)
    k = k_ref[...].astype(BF16)
    v = v_ref[...].astype(BF16)
    off = (qb - kb) * tq
    row = lax.broadcasted_iota(jnp.int32, (tq, tq), 0)
    col = lax.broadcasted_iota(jnp.int32, (tq, tq), 1)
    mask = col <= (row + off)
    for c in range(2):
        s = lax.dot_general(q[:, c * dqk:(c + 1) * dqk], k[:, c * dqk:(c + 1) * dqk], _NT,
                            preferred_element_type=F32) * scale
        s = jnp.where(mask, s, -jnp.inf)
        m_old = m_ref[c]
        m_new = jnp.maximum(m_old, jnp.max(s, axis=1, keepdims=True))
        alpha = jnp.exp(m_old - m_new)
        pr = jnp.exp(s - m_new)
        l_ref[c] = alpha * l_ref[c] + jnp.sum(pr, axis=1, keepdims=True)
        acc_ref[c] = alpha * acc_ref[c] + jnp.dot(pr.astype(BF16), v, preferred_element_type=F32)
        m_ref[c] = m_new

    @pl.when(kb == 0)
    def _():
        lam = _lambda(lam_ref[...], lam_init)
        o_ref[...] = acc_ref[0] / l_ref[0] - lam * (acc_ref[1] / l_ref[1])


def _diff_prompt(q, k, v, lam_p, lam_init, batch, heads, tq):
    t, w = q.shape
    hd = w // heads
    dqk = hd // 2
    nq = t // batch // tq
    qs, ks = _tri_pairs(nq)
    qspec = pl.BlockSpec((tq, hd), lambda b, h, p, qa, ka: (b * nq + qa[p], h))
    kspec = pl.BlockSpec((tq, hd), lambda b, h, p, qa, ka: (b * nq + ka[p], h))
    return pl.pallas_call(
        functools.partial(_diff_prompt_kernel, tq=tq, dqk=dqk, scale=1.0 / math.sqrt(dqk),
                          lam_init=lam_init),
        grid_spec=pltpu.PrefetchScalarGridSpec(
            num_scalar_prefetch=2,
            grid=(batch, heads, qs.shape[0]),
            in_specs=[qspec, kspec, kspec,
                      pl.BlockSpec(lam_p.shape, lambda b, h, p, qa, ka: (0, 0))],
            out_specs=qspec,
            scratch_shapes=[pltpu.VMEM((2, tq, hd), F32), pltpu.VMEM((2, tq, 1), F32),
                            pltpu.VMEM((2, tq, 1), F32)]),
        out_shape=jax.ShapeDtypeStruct((t, w), F32),
        compiler_params=_params("arbitrary", "arbitrary", "arbitrary"),
        name="diff_prompt",
    )(qs, ks, q, k, v, lam_p)


def _sample_attn_kernel(pt_ref, sbq_ref, dq_ref, dkn_ref, dvn_ref, *rest, heads, hd, lam_init, pps):
    ksb_refs, vsb_refs, kdt_refs, vd_refs = (rest[i * pps:(i + 1) * pps] for i in range(4))
    (lam_ref, osb_ref, od_ref, qsb_scr, qd_scr, acc_sb, c_scr, acc_d, m_scr, l_scr) = rest[4 * pps:]
    j = pl.program_id(1)
    w = heads * hd
    dqk = hd // 2
    rows = 2 * heads
    page = kdt_refs[0].shape[1]
    rr = lax.broadcasted_iota(jnp.int32, (rows, hd), 0)

    @pl.when(j == 0)
    def _():
        r_i = lax.broadcasted_iota(jnp.int32, (rows, w), 0)
        l_i = lax.broadcasted_iota(jnp.int32, (rows, w), 1)
        d_mask = (r_i % heads == l_i // hd) & (r_i // heads == (l_i // dqk) % 2)
        qd = jnp.where(d_mask, dq_ref[...], 0.0).astype(BF16)
        qd_scr[...] = qd
        qsb_scr[...] = jnp.concatenate([sbq_ref[...], jnp.zeros((heads, hd), F32)], axis=0).astype(BF16)
        acc_sb[...] = jnp.zeros_like(acc_sb)
        c_scr[...] = jnp.zeros_like(c_scr)
        kn = dkn_ref[...].astype(BF16).astype(F32)
        vn = dvn_ref[...].astype(BF16).astype(F32)
        m_scr[...] = jnp.sum(qd.astype(F32) * kn, axis=1, keepdims=True) * (1.0 / math.sqrt(dqk))
        l_scr[...] = jnp.ones_like(l_scr)
        acc_d[...] = jnp.concatenate([vn, vn], axis=0)

    def head_rows(ref, h):
        return ref[pl.ds(h, page, stride=heads), :].astype(BF16)

    def head_rows_all(refs, h):
        return jnp.concatenate([head_rows(r, h) for r in refs], axis=0)

    qsb = qsb_scr[...]
    rw = lax.broadcasted_iota(jnp.int32, (rows, pps * page), 0)

    z = jnp.zeros((rows, pps * page), F32)
    for h in range(heads):
        zh = lax.dot_general(qsb, head_rows_all(ksb_refs, h), _NT, preferred_element_type=F32)
        z = jnp.where(rw == h, zh, z)
    z = z * (1.0 / math.sqrt(hd))
    sp = _softplus(z)
    ls = z - sp
    chunks = [-sp[:, p * page:(p + 1) * page] for p in range(pps)]
    parts = [x for lk in chunks for x in _split_bf16(lk)]
    within = jnp.dot(jnp.concatenate(parts, axis=0), _suffix_matrix(page), preferred_element_type=F32)
    c = c_scr[...]
    a_parts = []
    for p in range(pps):
        after = within[2 * p * rows:(2 * p + 1) * rows] + within[(2 * p + 1) * rows:(2 * p + 2) * rows]
        a_parts.append(jnp.exp(ls[:, p * page:(p + 1) * page] + (after + c)))
        c = c + after[:, 0:1] + chunks[p][:, 0:1]
    c_scr[...] = c
    a = jnp.concatenate(a_parts, axis=1).astype(BF16)
    asb = acc_sb[...]
    for h in range(heads):
        oh = jnp.dot(a, head_rows_all(vsb_refs, h), preferred_element_type=F32)
        asb = asb + jnp.where(rr == h, oh, 0.0)
    acc_sb[...] = asb

    kdt = jnp.concatenate([r[...].astype(BF16) for r in kdt_refs], axis=1)
    s = jnp.dot(qd_scr[...], kdt, preferred_element_type=F32) * (1.0 / math.sqrt(dqk))
    m = m_scr[...]
    m_new = jnp.maximum(m, jnp.max(s, axis=1, keepdims=True))
    alpha = jnp.exp(m - m_new)
    pr = jnp.exp(s - m_new)
    l = alpha * l_scr[...] + jnp.sum(pr, axis=1, keepdims=True)
    pr = pr.astype(BF16)
    ad = alpha * acc_d[...]
    for h in range(heads):
        oh = jnp.dot(pr, head_rows_all(vd_refs, h), preferred_element_type=F32)
        ad = ad + jnp.where(rr % heads == h, oh, 0.0)
    m_scr[...], l_scr[...], acc_d[...] = m_new, l, ad

    @pl.when(j == pl.num_programs(1) - 1)
    def _():
        osb_ref[...] = asb[:heads, :]
        lam = _lambda(lam_ref[...], lam_init)
        nrm = ad / l
        od_ref[...] = nrm[:heads, :] - lam * nrm[heads:, :]


def _sample_attn(sbq, dq, dkn, dvn, c_sbk, c_sbv, c_dkt, c_dv, page_table, lam_p, lam_init, heads, pps):
    n, w = sbq.shape
    hd = w // heads
    npages = page_table.shape[1]
    page = c_dkt.shape[2]
    rows = 2 * heads
    assert npages % pps == 0
    row_spec = pl.BlockSpec((None, 1, w), lambda b, j, pt: (b, 0, 0))
    head_spec = pl.BlockSpec((None, heads, hd), lambda b, j, pt: (b, 0, 0))

    def cache_specs(shape):
        return [pl.BlockSpec((None,) + shape, lambda b, j, pt, p=p: (pt[b, npages - 1 - (j * pps + p)], 0, 0))
                for p in range(pps)]

    kv_specs = cache_specs((page * heads, hd))
    osb, od = pl.pallas_call(
        functools.partial(_sample_attn_kernel, heads=heads, hd=hd, lam_init=lam_init, pps=pps),
        grid_spec=pltpu.PrefetchScalarGridSpec(
            num_scalar_prefetch=1,
            grid=(n, npages // pps),
            in_specs=[head_spec, row_spec, row_spec, head_spec]
                     + kv_specs + kv_specs + cache_specs((w, page)) + kv_specs
                     + [pl.BlockSpec(lam_p.shape, lambda b, j, pt: (0, 0))],
            out_specs=[head_spec, head_spec],
            scratch_shapes=[pltpu.VMEM((rows, hd), BF16), pltpu.VMEM((rows, w), BF16),
                            pltpu.VMEM((rows, hd), F32), pltpu.VMEM((rows, 1), F32),
                            pltpu.VMEM((rows, hd), F32), pltpu.VMEM((rows, 1), F32),
                            pltpu.VMEM((rows, 1), F32)]),
        out_shape=[jax.ShapeDtypeStruct((n, heads, hd), F32)] * 2,
        compiler_params=_params("arbitrary", "arbitrary"),
        name="sample_attn",
    )(page_table, sbq.reshape(n, heads, hd), dq.reshape(n, 1, w), dkn.reshape(n, 1, w),
      dvn.reshape(n, heads, hd), *([c_sbk] * pps), *([c_sbv] * pps), *([c_dkt] * pps), *([c_dv] * pps),
      lam_p)
    return osb.reshape(n, w), od.reshape(n, w)


def _merge_kernel(sbo_ref, do_ref, x_ref, w_ref, dg_ref, gate_ref, g2_ref, sh_ref, sc_ref,
                  x2_ref, h2_ref, o_scr, *, heads, hd, lam_init):
    sbw = heads * hd
    o_scr[:, :sbw] = sbo_ref[...].astype(BF16)
    dg = dg_ref[...]
    for h in range(heads):
        d = do_ref[:, h * hd:(h + 1) * hd]
        ms = jnp.mean(d * d, axis=-1, keepdims=True)
        dn = (d * lax.rsqrt(ms + EPS) * dg) * (1.0 - lam_init)
        o_scr[:, sbw + h * hd:sbw + (h + 1) * hd] = dn.astype(BF16)
    o = jnp.dot(o_scr[...], w_ref[...], preferred_element_type=F32)
    x2 = x_ref[...] + gate_ref[...] * o
    x2_ref[...] = x2
    h2_ref[...] = _norm_mod(x2, g2_ref[...], sh_ref[...], sc_ref[...])


def _merge(sbo, do, x, w_bf, dg, g2, mods, lam_init, heads, tb, blocks_per_batch):
    t, d = x.shape
    nb, r, _ = mods.shape
    w = sbo.shape[1]
    hd = w // heads
    bmap = lambda i: i // blocks_per_batch
    mod_spec = lambda k: pl.BlockSpec((None, r, d), lambda i: (bmap(i), 0, k))
    tok = lambda width: pl.BlockSpec((tb, width), lambda i: (i, 0))
    return pl.pallas_call(
        functools.partial(_merge_kernel, heads=heads, hd=hd, lam_init=lam_init),
        grid=(t // tb,),
        in_specs=[tok(w), tok(w), tok(d),
                  pl.BlockSpec(w_bf.shape, lambda i: (0, 0)),
                  pl.BlockSpec((1, hd), lambda i: (0, 0)),
                  mod_spec(2),
                  pl.BlockSpec((1, d), lambda i: (0, 0)),
                  mod_spec(3), mod_spec(4)],
        out_specs=[tok(d), tok(d)],
        out_shape=[jax.ShapeDtypeStruct((t, d), F32)] * 2,
        scratch_shapes=[pltpu.VMEM((tb, 2 * w), BF16)],
        compiler_params=_params("arbitrary"),
        name="merge_outproj",
    )(sbo, do, x, w_bf, dg.reshape(1, hd), mods, g2.reshape(1, d), mods, mods)


def _topk_rows(s, k):
    n = s.shape[0]
    idx = lax.broadcasted_iota(jnp.int32, s.shape, 0).astype(F32)
    vals, ids = [], []
    for _ in range(k):
        m = jnp.max(s, axis=0, keepdims=True)
        i = jnp.min(jnp.where(s == m, idx, float(n)), axis=0, keepdims=True)
        s = jnp.where(idx == i, -jnp.inf, s)
        vals.append(m)
        ids.append(i)
    return jnp.concatenate(vals, axis=0), jnp.concatenate(ids, axis=0)


def _route_kernel(h_ref, whi_ref, wlo_ref, khi_ref, klo_ref, i_ref, j_ref, g_ref, *, half, topk):
    h_hi, h_lo = _split_bf16(h_ref[...])
    dot = functools.partial(jnp.dot, preferred_element_type=F32)
    q = dot(h_hi, whi_ref[...]) + dot(h_hi, wlo_ref[...]) + dot(h_lo, whi_ref[...])
    tops = []
    for c in range(2):
        q_hi, q_lo = _split_bf16(q[:, c * half:(c + 1) * half])
        k_hi, k_lo = khi_ref[c], klo_ref[c]
        nt = functools.partial(lax.dot_general, dimension_numbers=_NT, preferred_element_type=F32)
        s = nt(k_hi, q_hi) + nt(k_hi, q_lo) + nt(k_lo, q_hi)
        tops.append(_topk_rows(s, topk))
    (v0, i0), (v1, i1) = tops
    tb = v0.shape[1]
    sub8 = lax.broadcasted_iota(jnp.int32, (8, tb), 0).astype(F32)
    blocks, positions = [], []
    for a0, b in [(0, 0), (8, 0)] + [(0, b) for b in range(1, 8)]:
        blocks.append(v0[a0:a0 + 8, :] + v1[b:b + 1, :])
        positions.append((sub8 + float(a0)) * float(topk) + float(b))
    blocks.append(v0[0:1, :] + v1[8:16, :])
    positions.append(sub8 + 8.0)
    cand = jnp.stack(blocks, axis=0)
    pos = jnp.stack(positions, axis=0)
    a_iota = lax.broadcasted_iota(jnp.int32, (topk, tb), 0).astype(F32)
    best, sel_i, sel_j = [], [], []
    for _ in range(topk):
        m = jnp.max(jnp.max(cand, axis=0), axis=0, keepdims=True)
        p = jnp.min(jnp.min(jnp.where(cand == m[None], pos, float(topk * topk)), axis=0),
                    axis=0, keepdims=True)
        cand = jnp.where(pos == p[None], -jnp.inf, cand)
        pa = jnp.floor(p * (1.0 / topk))
        pb = p - pa * topk
        best.append(m)
        sel_i.append(jnp.sum(jnp.where(a_iota == pa, i0, 0.0), axis=0, keepdims=True))
        sel_j.append(jnp.sum(jnp.where(a_iota == pb, i1, 0.0), axis=0, keepdims=True))
    best = jnp.concatenate(best, axis=0)
    e = jnp.exp(best - best[0:1, :])
    g_ref[...] = e / jnp.sum(e, axis=0, keepdims=True)
    i_ref[...] = jnp.concatenate(sel_i, axis=0).astype(jnp.int32)
    j_ref[...] = jnp.concatenate(sel_j, axis=0).astype(jnp.int32)


def _route(h, wq_hi, wq_lo, keys_hi, keys_lo, tb):
    t, d = h.shape
    heads, _, n_keys, half = keys_hi.shape
    topk = PEER_TOPK
    assert topk == 16
    out_spec = pl.BlockSpec((topk, tb), lambda i, hh: (hh, i))
    key_spec = pl.BlockSpec((None, 2, n_keys, half), lambda i, hh: (hh, 0, 0, 0))
    w_spec = pl.BlockSpec((d, 2 * half), lambda i, hh: (0, hh))
    return pl.pallas_call(
        functools.partial(_route_kernel, half=half, topk=topk),
        grid=(t // tb, heads),
        in_specs=[pl.BlockSpec((tb, d), lambda i, hh: (i, 0)), w_spec, w_spec, key_spec, key_spec],
        out_specs=[out_spec] * 3,
        out_shape=[jax.ShapeDtypeStruct((heads * topk, t), jnp.int32)] * 2
                  + [jax.ShapeDtypeStruct((heads * topk, t), F32)],
        compiler_params=_params("arbitrary", "arbitrary"),
        name="peer_route",
    )(h, wq_hi, wq_lo, keys_hi, keys_lo)


def _gelu(x):
    return 0.5 * x * (1.0 + lax.erf(x * (1.0 / math.sqrt(2.0))))


def _peer_kernel(h_ref, i_ref, j_ref, gt_ref, u_ref, v_ref, x_ref, gate_ref, gf_ref, sh_ref, sc_ref,
                 y_ref, gmat, hb_scr, wd_scr, acc_ref, *, tb, n_keys, eb):
    e = pl.program_id(1)
    per = PEER_CHUNK // n_keys

    @pl.when(e == 0)
    def _():
        hb_scr[...] = h_ref[...].astype(BF16)
        acc_ref[...] = jnp.zeros_like(acc_ref)
        sub = lax.broadcasted_iota(jnp.int32, (n_keys, n_keys), 0)

        def build_one(t):
            irow = i_ref[pl.ds(t, 1), :]
            jrow = j_ref[pl.ds(t, 1), :]
            g = gt_ref[pl.ds(t, 1), :]
            g_hi = g.astype(BF16).astype(F32)
            hit_i = sub == irow
            a = jnp.concatenate([jnp.where(hit_i, g_hi, 0.0), jnp.where(hit_i, g - g_hi, 0.0)],
                                axis=1).astype(BF16)
            b = jnp.where(sub == jrow, 1.0, 0.0).astype(BF16)
            return lax.dot_general(a, jnp.concatenate([b, b], axis=1), _NT,
                                   preferred_element_type=F32)

        def build(o, carry):
            un = PEER_BUILD_UNROLL
            g = jnp.stack([build_one(o * un + k) for k in range(un)], axis=0)
            gmat[:, pl.ds(pl.multiple_of(o * un, un), un), :] = pltpu.einshape("tij->itj", g)
            return carry

        lax.fori_loop(0, tb // PEER_BUILD_UNROLL, build, 0)

    hb = hb_scr[...]
    for cix in range(eb // PEER_CHUNK):
        lo = cix * PEER_CHUNK
        s = lax.dot_general(hb, u_ref[lo:lo + PEER_CHUNK, :], _NT, preferred_element_type=F32)
        first = e * (eb // n_keys) + cix * per
        gate = jnp.concatenate(
            [gmat[first + r] for r in range(per)], axis=1)
        wd_scr[:, lo:lo + PEER_CHUNK] = (gate * _gelu(s)).astype(BF16)
    acc_ref[...] += jnp.dot(wd_scr[...], v_ref[...], preferred_element_type=F32)

    @pl.when(e == pl.num_programs(1) - 1)
    def _():
        x3 = x_ref[...] + gate_ref[...] * acc_ref[...]
        y_ref[...] = _norm_mod(x3, gf_ref[...], sh_ref[...], sc_ref[...])


def _peer(h, i_sel, j_sel, g_sel, u_bf, v_bf, x, mods, fmods, gf, tb, blocks_per_batch):
    t, d = h.shape
    n_exp = u_bf.shape[0]
    n_keys = PEER_N_KEYS
    eb = PEER_EXPERT_BLOCK
    npick = i_sel.shape[1]
    nb, r, _ = mods.shape
    assert tb % PEER_BUILD_UNROLL == 0 and eb % PEER_CHUNK == 0 and PEER_CHUNK % n_keys == 0
    bmap = lambda i: i // blocks_per_batch
    tok = lambda width: pl.BlockSpec((tb, width), lambda i, e: (i, 0))
    tab = pl.BlockSpec((eb, d), lambda i, e: (e, 0))
    return pl.pallas_call(
        functools.partial(_peer_kernel, tb=tb, n_keys=n_keys, eb=eb),
        grid=(t // tb, n_exp // eb),
        in_specs=[tok(d), tok(npick), tok(npick), tok(npick), tab, tab, tok(d),
                  pl.BlockSpec((None, r, d), lambda i, e: (bmap(i), 0, 5)),
                  pl.BlockSpec((1, d), lambda i, e: (0, 0)),
                  pl.BlockSpec((None, r, d), lambda i, e: (bmap(i), 0, 0)),
                  pl.BlockSpec((None, r, d), lambda i, e: (bmap(i), 0, 1))],
        out_specs=tok(d),
        out_shape=jax.ShapeDtypeStruct((t, d), F32),
        scratch_shapes=[pltpu.VMEM((n_keys, tb, n_keys), F32), pltpu.VMEM((tb, d), BF16),
                        pltpu.VMEM((tb, eb), BF16), pltpu.VMEM((tb, d), F32)],
        compiler_params=_params("arbitrary", "arbitrary"),
        name="peer_mix",
    )(h, i_sel, j_sel, g_sel, u_bf, v_bf, x, mods, gf.reshape(1, d), fmods, fmods)


def _token_path(x, mods, fmods, pos, tb, blocks_per_batch, attend, w):
    proj = _inproj(x, w["norm_mix_g"], mods, w["w_in"], pos, tb, blocks_per_batch)
    sbo, do = attend(*proj)
    x2, h2 = _merge(sbo, do, x, w["w_out"], w["diff_norm_g"], w["norm_ffn_g"], mods,
                    w["lam_init"], w["heads"], tb, blocks_per_batch)
    i_sel, j_sel, g_sel = _route(h2, w["wq_hi"], w["wq_lo"], w["keys_hi"], w["keys_lo"], tb)
    y = _peer(h2, i_sel.T, j_sel.T, g_sel.T, w["u"], w["v"], x2, mods, fmods,
              w["norm_final_g"], tb, blocks_per_batch)
    return y, proj


def kernel(x_prompt, x_sample, c_prompt, c_sample, cache_sb_k, cache_sb_v, cache_diff_k, cache_diff_v, page_table, norm_mix_g, norm_ffn_g, w_ada, b_ada, w_in, diff_lambda, diff_norm_g, w_out, peer_w_q, peer_sub_keys, peer_u, peer_v, norm_final_g, w_ada_final, b_ada_final):
    batch, seq, d = x_prompt.shape
    nsamp, dec_seq, _ = x_sample.shape
    depth = w_ada.shape[0]
    assert depth == 1 and dec_seq == 1
    heads = cache_sb_k.shape[3]
    hd = cache_sb_k.shape[4]
    past_len = page_table.shape[1] * cache_sb_k.shape[2]
    l = 0
    lam_init = 0.8 - 0.6 * math.exp(-0.3 * l)

    nc = batch + nsamp
    c_all = jnp.concatenate([c_prompt, c_sample, jnp.zeros((-nc % 8, d), F32)], axis=0)
    mods = _adaln(c_all, w_ada[l], b_ada[l])
    fmods = _adaln(c_all, w_ada_final, b_ada_final)
    mods_p, mods_s = mods[:batch, None, :], mods[None, batch:nc, :]
    fmods_p, fmods_s = fmods[:batch, None, :], fmods[None, batch:nc, :]

    wq_hi, wq_lo = _split_bf16(peer_w_q[l])
    keys_hi, keys_lo = _split_bf16(peer_sub_keys[l])
    w = dict(norm_mix_g=norm_mix_g[l], norm_ffn_g=norm_ffn_g[l], w_in=w_in[l].astype(BF16),
             w_out=w_out[l].astype(BF16), diff_norm_g=diff_norm_g[l], lam_init=lam_init, heads=heads,
             wq_hi=wq_hi, wq_lo=wq_lo, keys_hi=keys_hi, keys_lo=keys_lo,
             u=peer_u[l].astype(BF16), v=peer_v[l].astype(BF16), norm_final_g=norm_final_g)
    lam_p = diff_lambda[l]

    def attend_prompt(sb_q, sb_k, sb_v, d_q, d_k, d_v):
        return (_sb_prompt(sb_q, sb_k, sb_v, batch, heads, ATTN_BLOCK, SB_CHUNK),
                _diff_prompt(d_q, d_k, d_v, lam_p, lam_init, batch, heads, ATTN_BLOCK))

    def attend_sample(sb_q, sb_k, sb_v, d_q, d_k, d_v):
        pool, page = cache_sb_k.shape[1], cache_sb_k.shape[2]
        rows = lambda c: c.reshape(pool, page * heads, hd)
        dkt = jnp.transpose(cache_diff_k, (0, 1, 3, 4, 5, 2)).reshape(pool, heads * hd, page)
        return _sample_attn(sb_q, d_q, d_k, d_v, rows(cache_sb_k), rows(cache_sb_v), dkt,
                            rows(cache_diff_v), page_table, lam_p, lam_init, heads,
                            math.gcd(PAGES_PER_STEP, page_table.shape[1]))

    tb_p = TOKEN_BLOCK
    y_p, proj_p = _token_path(x_prompt.reshape(batch * seq, d), mods_p, fmods_p, jnp.arange(seq),
                              tb_p, seq // tb_p, attend_prompt, w)
    y_s, proj_s = _token_path(x_sample.reshape(nsamp, d), mods_s, fmods_s,
                              jnp.full((nsamp,), past_len, jnp.int32), nsamp, 1, attend_sample, w)

    def rows(proj, n, t):
        _, sb_k, sb_v, _, d_k, d_v = proj
        return (sb_k.reshape(1, n, t, heads, hd), sb_v.reshape(1, n, t, heads, hd),
                d_k.reshape(1, n, t, heads, 2, hd // 2), d_v.reshape(1, n, t, heads, hd))

    return ((y_p.reshape(batch, seq, d), y_s.reshape(nsamp, dec_seq, d))
            + rows(proj_p, batch, seq) + rows(proj_s, nsamp, dec_seq))
```

```python
import functools
import math

import jax
import jax.numpy as jnp
import numpy as np
from jax import lax
from jax.experimental import pallas as pl
from jax.experimental.pallas import tpu as pltpu

F32 = jnp.float32
BF16 = jnp.bfloat16

EPS = 1e-6
ROPE_THETA = 500000.0
PEER_N_KEYS = 128
PEER_TOPK = 16
PEER_HEADS = 8
LANES = 128
VMEM_LIMIT = 56 * 1024 * 1024
TOKEN_BLOCK = 256
ATTN_BLOCK = 512
SB_CHUNK = 256
PAGES_PER_STEP = 4
PEER_EXPERT_BLOCK = 1024
PEER_CHUNK = 256
PEER_BUILD_UNROLL = 8

_NT = (((1,), (1,)), ((), ()))


def _params(*sem):
    return pltpu.CompilerParams(dimension_semantics=sem, vmem_limit_bytes=VMEM_LIMIT)


def _split_bf16(x):
    hi = x.astype(BF16)
    lo = (x - hi.astype(F32)).astype(BF16)
    return hi, lo


def _norm_mod(x, g, shift, scale):
    ms = jnp.mean(x * x, axis=-1, keepdims=True)
    y = x * lax.rsqrt(ms + EPS) * g
    return y * (1.0 + scale) + shift


def _adaln_kernel(c_ref, w_ref, b_ref, o_ref):
    c = c_ref[...]
    a = (c * jax.nn.sigmoid(c)).astype(BF16)
    o_ref[...] = jnp.dot(a, w_ref[...].astype(BF16), preferred_element_type=F32) + b_ref[...]


def _adaln(c, w, b, tn=1024):
    m, d = c.shape
    n = w.shape[1]
    return pl.pallas_call(
        _adaln_kernel,
        grid=(n // tn,),
        in_specs=[pl.BlockSpec((m, d), lambda j: (0, 0)),
                  pl.BlockSpec((d, tn), lambda j: (0, j)),
                  pl.BlockSpec((1, tn), lambda j: (0, j))],
        out_specs=pl.BlockSpec((m, tn), lambda j: (0, j)),
        out_shape=jax.ShapeDtypeStruct((m, n), F32),
        compiler_params=_params("arbitrary"),
        name="adaln",
    )(c, w, b.reshape(1, n))


def _rope_tables(pos):
    rot = 16
    lane = np.arange(LANES)
    sub = lane % 64
    inv = ROPE_THETA ** (-jnp.arange(0, rot, 2, dtype=F32) / rot)
    ang = pos.astype(F32)[:, None] * inv
    ang = ang[:, sub % 8]
    cos, sin = jnp.cos(ang), jnp.sin(ang)
    first = jnp.asarray(sub < 8)
    second = jnp.asarray((sub >= 8) & (sub < 16))
    c = jnp.where(first | second, cos, 1.0)
    s_lo = jnp.where(first, -sin, 0.0)
    s_hi = jnp.where(second, sin, 0.0)
    return c, s_lo, s_hi


def _inproj_kernel(x_ref, g_ref, sh_ref, sc_ref, w_ref, cos_ref, slo_ref, shi_ref,
                   o0, o1, o2, o3, o4, o5, h_scr):
    n = pl.program_id(1)

    @pl.when(n == 0)
    def _():
        h_scr[...] = _norm_mod(x_ref[...], g_ref[...], sh_ref[...], sc_ref[...]).astype(BF16)

    for idx, o in enumerate((o0, o1, o2, o3, o4, o5)):
        @pl.when(n == idx)
        def _(o=o, idx=idx):
            if idx in (3, 4):
                cw = 2 * LANES
                cos, slo, shi = (jnp.tile(r[...], (1, cw // LANES)) for r in (cos_ref, slo_ref, shi_ref))
                for j in range(w_ref.shape[1] // cw):
                    zc = jnp.dot(h_scr[...], w_ref[:, j * cw:(j + 1) * cw], preferred_element_type=F32)
                    o[:, j * cw:(j + 1) * cw] = (
                        zc * cos + pltpu.roll(zc, cw - 8, 1) * slo + pltpu.roll(zc, 8, 1) * shi)
            else:
                o[...] = jnp.dot(h_scr[...], w_ref[...], preferred_element_type=F32)


def _inproj(x, g, mods, w_bf, pos, tb, blocks_per_batch):
    t, d = x.shape
    nb, r, _ = mods.shape
    wn = 1024
    nblk = w_bf.shape[1] // wn
    assert nblk == 6
    cos, slo, shi = _rope_tables(pos)
    npos = pos.shape[0] // tb
    bmap = lambda i: i // blocks_per_batch
    mod_spec = lambda k: pl.BlockSpec((None, r, d), lambda i, n: (bmap(i), 0, k))
    tab_spec = pl.BlockSpec((tb, LANES), lambda i, n: (i % npos, 0))
    out_spec = pl.BlockSpec((tb, wn), lambda i, n: (i, 0))
    return pl.pallas_call(
        _inproj_kernel,
        grid=(t // tb, nblk),
        in_specs=[pl.BlockSpec((tb, d), lambda i, n: (i, 0)),
                  pl.BlockSpec((1, d), lambda i, n: (0, 0)),
                  mod_spec(0), mod_spec(1),
                  pl.BlockSpec((d, wn), lambda i, n: (0, n)),
                  tab_spec, tab_spec, tab_spec],
        out_specs=[out_spec] * 6,
        out_shape=[jax.ShapeDtypeStruct((t, wn), F32)] * 6,
        scratch_shapes=[pltpu.VMEM((tb, d), BF16)],
        compiler_params=_params("arbitrary", "arbitrary"),
        name="inproj",
    )(x, g.reshape(1, d), mods, mods, w_bf, cos, slo, shi)


def _softplus(z):
    return jnp.maximum(z, 0.0) + jnp.log(1.0 + jnp.exp(-jnp.abs(z)))


def _suffix_matrix(n):
    s = lax.broadcasted_iota(jnp.int32, (n, n), 0)
    j = lax.broadcasted_iota(jnp.int32, (n, n), 1)
    return jnp.where(s > j, 1.0, 0.0).astype(BF16)


def _lane_tile(x, width):
    return jnp.tile(x, (1, width // LANES))


def _sb_chunk(z, v_bf, c, mask):
    sp = _softplus(z)
    lk = -sp if mask is None else jnp.where(mask, -sp, 0.0)
    lk_hi, lk_lo = _split_bf16(lk)
    u = _suffix_matrix(z.shape[1])
    after = (jnp.dot(lk_hi, u, preferred_element_type=F32)
             + jnp.dot(lk_lo, u, preferred_element_type=F32))
    a = jnp.exp((z - sp) + (after + _lane_tile(c, z.shape[1])))
    if mask is not None:
        a = jnp.where(mask, a, 0.0)
    contrib = jnp.dot(a.astype(BF16), v_bf, preferred_element_type=F32)
    return contrib, c + (after[:, 0:1] + lk[:, 0:1])


def _sb_prompt_kernel(qb_ref, kb_ref, q_ref, k_ref, v_ref, o_ref, acc_ref, c_ref, *, tq, sub, scale):
    p = pl.program_id(2)
    qb, kb = qb_ref[p], kb_ref[p]

    def step(diagonal):
        q = q_ref[...].astype(BF16)
        if diagonal:
            acc, c = jnp.zeros(acc_ref.shape, F32), jnp.zeros(c_ref.shape, F32)
            row = lax.broadcasted_iota(jnp.int32, (tq, sub), 0)
            col = lax.broadcasted_iota(jnp.int32, (tq, sub), 1)
        else:
            acc, c = acc_ref[...], c_ref[...]
        for s in reversed(range(tq // sub)):
            k = k_ref[s * sub:(s + 1) * sub, :].astype(BF16)
            v = v_ref[s * sub:(s + 1) * sub, :].astype(BF16)
            z = lax.dot_general(q, k, _NT, preferred_element_type=F32) * scale
            mask = (col + s * sub) < row if diagonal else None
            contrib, c = _sb_chunk(z, v, c, mask)
            acc = acc + contrib
        acc_ref[...] = acc
        c_ref[...] = c

    pl.when(kb == qb)(lambda: step(True))
    pl.when(kb != qb)(lambda: step(False))

    @pl.when(kb == 0)
    def _():
        o_ref[...] = acc_ref[...]


def _tri_pairs(nq):
    qs, ks = [], []
    for q in range(nq):
        for k in range(q, -1, -1):
            qs.append(q)
            ks.append(k)
    return jnp.asarray(qs, jnp.int32), jnp.asarray(ks, jnp.int32)


def _sb_prompt(q, k, v, batch, heads, tq, sub):
    t, w = q.shape
    hd = w // heads
    nq = t // batch // tq
    qs, ks = _tri_pairs(nq)
    qspec = pl.BlockSpec((tq, hd), lambda b, h, p, qa, ka: (b * nq + qa[p], h))
    kspec = pl.BlockSpec((tq, hd), lambda b, h, p, qa, ka: (b * nq + ka[p], h))
    return pl.pallas_call(
        functools.partial(_sb_prompt_kernel, tq=tq, sub=sub, scale=1.0 / math.sqrt(hd)),
        grid_spec=pltpu.PrefetchScalarGridSpec(
            num_scalar_prefetch=2,
            grid=(batch, heads, qs.shape[0]),
            in_specs=[qspec, kspec, kspec],
            out_specs=qspec,
            scratch_shapes=[pltpu.VMEM((tq, hd), F32), pltpu.VMEM((tq, LANES), F32)]),
        out_shape=jax.ShapeDtypeStruct((t, w), F32),
        compiler_params=_params("arbitrary", "arbitrary", "arbitrary"),
        name="sb_prompt",
    )(qs, ks, q, k, v)


def _lambda(lp, lam_init):
    a = jnp.sum(lp[0:1, :] * lp[1:2, :], axis=1, keepdims=True)
    b = jnp.sum(lp[2:3, :] * lp[3:4, :], axis=1, keepdims=True)
    return jnp.exp(a) - jnp.exp(b) + lam_init


def _diff_prompt_kernel(qb_ref, kb_ref, q_ref, k_ref, v_ref, lam_ref, o_ref,
                        q2_scr, acc_ref, m_ref, l_ref, *, tq, dqk, scale, lam_init):
    p = pl.program_id(2)
    qb, kb = qb_ref[p], kb_ref[p]
    fold = math.frexp(scale)[0] == 0.5

    def step(diagonal):
        if diagonal:
            q = q_ref[...] * scale if fold else q_ref[...]
            lane = lax.broadcasted_iota(jnp.int32, q.shape, 1)
            q2 = jnp.concatenate([jnp.where(lane < dqk, q, 0.0), jnp.where(lane >= dqk, q, 0.0)],
                                 axis=0).astype(BF16)
            q2_scr[...] = q2
        else:
            q2 = q2_scr[...]
        s = lax.dot_general(q2, k_ref[...].astype(BF16), _NT, preferred_element_type=F32)
        if not fold:
            s = s * scale
        v = v_ref[...].astype(BF16)
        if diagonal:
            row = lax.broadcasted_iota(jnp.int32, s.shape, 0)
            col = lax.broadcasted_iota(jnp.int32, s.shape, 1)
            s = jnp.where(col <= jnp.where(row >= tq, row - tq, row), s, -jnp.inf)
            m_new = jnp.broadcast_to(jnp.max(s, axis=1, keepdims=True), m_ref.shape)
            pr = jnp.exp(s - _lane_tile(m_new, tq))
            l_ref[...] = jnp.broadcast_to(jnp.sum(pr, axis=1, keepdims=True), l_ref.shape)
            acc_ref[...] = jnp.dot(pr.astype(BF16), v, preferred_element_type=F32)
        else:
            m_old = m_ref[...]
            m_new = jnp.maximum(m_old, jnp.max(s, axis=1, keepdims=True))
            alpha = jnp.exp(m_old - m_new)
            pr = jnp.exp(s - _lane_tile(m_new, tq))
            l_ref[...] = alpha * l_ref[...] + jnp.sum(pr, axis=1, keepdims=True)
            acc_ref[...] = alpha * acc_ref[...] + jnp.dot(pr.astype(BF16), v, preferred_element_type=F32)
        m_ref[...] = m_new

    pl.when(kb == qb)(lambda: step(True))
    pl.when(kb != qb)(lambda: step(False))

    @pl.when(kb == 0)
    def _():
        lam = _lambda(lam_ref[...], lam_init)
        nrm = acc_ref[...] / l_ref[...]
        o_ref[...] = nrm[:tq] - lam * nrm[tq:]


def _diff_prompt(q, k, v, lam_p, lam_init, batch, heads, tq):
    t, w = q.shape
    hd = w // heads
    dqk = hd // 2
    nq = t // batch // tq
    qs, ks = _tri_pairs(nq)
    qspec = pl.BlockSpec((tq, hd), lambda b, h, p, qa, ka: (b * nq + qa[p], h))
    kspec = pl.BlockSpec((tq, hd), lambda b, h, p, qa, ka: (b * nq + ka[p], h))
    return pl.pallas_call(
        functools.partial(_diff_prompt_kernel, tq=tq, dqk=dqk, scale=1.0 / math.sqrt(dqk),
                          lam_init=lam_init),
        grid_spec=pltpu.PrefetchScalarGridSpec(
            num_scalar_prefetch=2,
            grid=(batch, heads, qs.shape[0]),
            in_specs=[qspec, kspec, kspec,
                      pl.BlockSpec(lam_p.shape, lambda b, h, p, qa, ka: (0, 0))],
            out_specs=qspec,
            scratch_shapes=[pltpu.VMEM((2 * tq, hd), BF16), pltpu.VMEM((2 * tq, hd), F32),
                            pltpu.VMEM((2 * tq, LANES), F32), pltpu.VMEM((2 * tq, LANES), F32)]),
        out_shape=jax.ShapeDtypeStruct((t, w), F32),
        compiler_params=_params("arbitrary", "arbitrary", "arbitrary"),
        name="diff_prompt",
    )(qs, ks, q, k, v, lam_p)


def _sample_attn_kernel(pt_ref, sbq_ref, dq_ref, dkn_ref, dvn_ref, *rest, heads, hd, lam_init, pps):
    ksb_refs, vsb_refs, kdt_refs, vd_refs = (rest[i * pps:(i + 1) * pps] for i in range(4))
    (lam_ref, osb_ref, od_ref, qsb_scr, qd_scr, acc_sb, c_scr, acc_d, m_scr, l_scr) = rest[4 * pps:]
    j = pl.program_id(1)
    w = heads * hd
    dqk = hd // 2
    rows = 2 * heads
    page = kdt_refs[0].shape[1]
    rr = lax.broadcasted_iota(jnp.int32, (rows, hd), 0)

    @pl.when(j == 0)
    def _():
        r_i = lax.broadcasted_iota(jnp.int32, (rows, w), 0)
        l_i = lax.broadcasted_iota(jnp.int32, (rows, w), 1)
        d_mask = (r_i % heads == l_i // hd) & (r_i // heads == (l_i // dqk) % 2)
        qd = jnp.where(d_mask, dq_ref[...], 0.0).astype(BF16)
        qd_scr[...] = qd
        qsb_scr[...] = jnp.concatenate([sbq_ref[...], jnp.zeros((heads, hd), F32)], axis=0).astype(BF16)
        acc_sb[...] = jnp.zeros_like(acc_sb)
        c_scr[...] = jnp.zeros_like(c_scr)
        kn = dkn_ref[...].astype(BF16).astype(F32)
        vn = dvn_ref[...].astype(BF16).astype(F32)
        m_scr[...] = jnp.sum(qd.astype(F32) * kn, axis=1, keepdims=True) * (1.0 / math.sqrt(dqk))
        l_scr[...] = jnp.ones_like(l_scr)
        acc_d[...] = jnp.concatenate([vn, vn], axis=0)

    def head_rows(ref, h):
        return ref[pl.ds(h, page, stride=heads), :].astype(BF16)

    def head_rows_all(refs, h):
        return jnp.concatenate([head_rows(r, h) for r in refs], axis=0)

    qsb = qsb_scr[...]
    rw = lax.broadcasted_iota(jnp.int32, (rows, pps * page), 0)

    z = jnp.zeros((rows, pps * page), F32)
    for h in range(heads):
        zh = lax.dot_general(qsb, head_rows_all(ksb_refs, h), _NT, preferred_element_type=F32)
        z = jnp.where(rw == h, zh, z)
    z = z * (1.0 / math.sqrt(hd))
    sp = _softplus(z)
    ls = z - sp
    chunks = [-sp[:, p * page:(p + 1) * page] for p in range(pps)]
    parts = [x for lk in chunks for x in _split_bf16(lk)]
    within = jnp.dot(jnp.concatenate(parts, axis=0), _suffix_matrix(page), preferred_element_type=F32)
    c = c_scr[...]
    a_parts = []
    for p in range(pps):
        after = within[2 * p * rows:(2 * p + 1) * rows] + within[(2 * p + 1) * rows:(2 * p + 2) * rows]
        a_parts.append(jnp.exp(ls[:, p * page:(p + 1) * page] + (after + c)))
        c = c + after[:, 0:1] + chunks[p][:, 0:1]
    c_scr[...] = c
    a = jnp.concatenate(a_parts, axis=1).astype(BF16)
    asb = acc_sb[...]
    for h in range(heads):
        oh = jnp.dot(a, head_rows_all(vsb_refs, h), preferred_element_type=F32)
        asb = asb + jnp.where(rr == h, oh, 0.0)
    acc_sb[...] = asb

    kdt = jnp.concatenate([r[...].astype(BF16) for r in kdt_refs], axis=1)
    s = jnp.dot(qd_scr[...], kdt, preferred_element_type=F32) * (1.0 / math.sqrt(dqk))
    m = m_scr[...]
    m_new = jnp.maximum(m, jnp.max(s, axis=1, keepdims=True))
    alpha = jnp.exp(m - m_new)
    pr = jnp.exp(s - m_new)
    l = alpha * l_scr[...] + jnp.sum(pr, axis=1, keepdims=True)
    pr = pr.astype(BF16)
    ad = alpha * acc_d[...]
    for h in range(heads):
        oh = jnp.dot(pr, head_rows_all(vd_refs, h), preferred_element_type=F32)
        ad = ad + jnp.where(rr % heads == h, oh, 0.0)
    m_scr[...], l_scr[...], acc_d[...] = m_new, l, ad

    @pl.when(j == pl.num_programs(1) - 1)
    def _():
        osb_ref[...] = asb[:heads, :]
        lam = _lambda(lam_ref[...], lam_init)
        nrm = ad / l
        od_ref[...] = nrm[:heads, :] - lam * nrm[heads:, :]


def _sample_attn(sbq, dq, dkn, dvn, c_sbk, c_sbv, c_dkt, c_dv, page_table, lam_p, lam_init, heads, pps):
    n, w = sbq.shape
    hd = w // heads
    npages = page_table.shape[1]
    page = c_dkt.shape[2]
    rows = 2 * heads
    assert npages % pps == 0
    row_spec = pl.BlockSpec((None, 1, w), lambda b, j, pt: (b, 0, 0))
    head_spec = pl.BlockSpec((None, heads, hd), lambda b, j, pt: (b, 0, 0))

    def cache_specs(shape):
        return [pl.BlockSpec((None,) + shape, lambda b, j, pt, p=p: (pt[b, npages - 1 - (j * pps + p)], 0, 0))
                for p in range(pps)]

    kv_specs = cache_specs((page * heads, hd))
    osb, od = pl.pallas_call(
        functools.partial(_sample_attn_kernel, heads=heads, hd=hd, lam_init=lam_init, pps=pps),
        grid_spec=pltpu.PrefetchScalarGridSpec(
            num_scalar_prefetch=1,
            grid=(n, npages // pps),
            in_specs=[head_spec, row_spec, row_spec, head_spec]
                     + kv_specs + kv_specs + cache_specs((w, page)) + kv_specs
                     + [pl.BlockSpec(lam_p.shape, lambda b, j, pt: (0, 0))],
            out_specs=[head_spec, head_spec],
            scratch_shapes=[pltpu.VMEM((rows, hd), BF16), pltpu.VMEM((rows, w), BF16),
                            pltpu.VMEM((rows, hd), F32), pltpu.VMEM((rows, 1), F32),
                            pltpu.VMEM((rows, hd), F32), pltpu.VMEM((rows, 1), F32),
                            pltpu.VMEM((rows, 1), F32)]),
        out_shape=[jax.ShapeDtypeStruct((n, heads, hd), F32)] * 2,
        compiler_params=_params("arbitrary", "arbitrary"),
        name="sample_attn",
    )(page_table, sbq.reshape(n, heads, hd), dq.reshape(n, 1, w), dkn.reshape(n, 1, w),
      dvn.reshape(n, heads, hd), *([c_sbk] * pps), *([c_sbv] * pps), *([c_dkt] * pps), *([c_dv] * pps),
      lam_p)
    return osb.reshape(n, w), od.reshape(n, w)


def _merge_kernel(sbo_ref, do_ref, x_ref, w_ref, dg_ref, gate_ref, g2_ref, sh_ref, sc_ref,
                  x2_ref, h2_ref, o_scr, *, heads, hd, lam_init):
    sbw = heads * hd
    o_scr[:, :sbw] = sbo_ref[...].astype(BF16)
    dg = dg_ref[...]
    for h in range(heads):
        d = do_ref[:, h * hd:(h + 1) * hd]
        ms = jnp.mean(d * d, axis=-1, keepdims=True)
        dn = (d * lax.rsqrt(ms + EPS) * dg) * (1.0 - lam_init)
        o_scr[:, sbw + h * hd:sbw + (h + 1) * hd] = dn.astype(BF16)
    o = jnp.dot(o_scr[...], w_ref[...], preferred_element_type=F32)
    x2 = x_ref[...] + gate_ref[...] * o
    x2_ref[...] = x2
    h2_ref[...] = _norm_mod(x2, g2_ref[...], sh_ref[...], sc_ref[...])


def _merge(sbo, do, x, w_bf, dg, g2, mods, lam_init, heads, tb, blocks_per_batch):
    t, d = x.shape
    nb, r, _ = mods.shape
    w = sbo.shape[1]
    hd = w // heads
    bmap = lambda i: i // blocks_per_batch
    mod_spec = lambda k: pl.BlockSpec((None, r, d), lambda i: (bmap(i), 0, k))
    tok = lambda width: pl.BlockSpec((tb, width), lambda i: (i, 0))
    return pl.pallas_call(
        functools.partial(_merge_kernel, heads=heads, hd=hd, lam_init=lam_init),
        grid=(t // tb,),
        in_specs=[tok(w), tok(w), tok(d),
                  pl.BlockSpec(w_bf.shape, lambda i: (0, 0)),
                  pl.BlockSpec((1, hd), lambda i: (0, 0)),
                  mod_spec(2),
                  pl.BlockSpec((1, d), lambda i: (0, 0)),
                  mod_spec(3), mod_spec(4)],
        out_specs=[tok(d), tok(d)],
        out_shape=[jax.ShapeDtypeStruct((t, d), F32)] * 2,
        scratch_shapes=[pltpu.VMEM((tb, 2 * w), BF16)],
        compiler_params=_params("arbitrary"),
        name="merge_outproj",
    )(sbo, do, x, w_bf, dg.reshape(1, hd), mods, g2.reshape(1, d), mods, mods)


def _topk_rows(s, k):
    n = s.shape[0]
    idx = lax.broadcasted_iota(jnp.int32, s.shape, 0).astype(F32)
    vals, ids = [], []
    for _ in range(k):
        m = jnp.max(s, axis=0, keepdims=True)
        i = jnp.min(jnp.where(s == m, idx, float(n)), axis=0, keepdims=True)
        s = jnp.where(idx == i, -jnp.inf, s)
        vals.append(m)
        ids.append(i)
    return jnp.concatenate(vals, axis=0), jnp.concatenate(ids, axis=0)


def _route_kernel(h_ref, whi_ref, wlo_ref, khi_ref, klo_ref, i_ref, j_ref, g_ref, *, half, topk):
    h_hi, h_lo = _split_bf16(h_ref[...])
    dot = functools.partial(jnp.dot, preferred_element_type=F32)
    q = dot(h_hi, whi_ref[...]) + dot(h_hi, wlo_ref[...]) + dot(h_lo, whi_ref[...])
    tops = []
    for c in range(2):
        q_hi, q_lo = _split_bf16(q[:, c * half:(c + 1) * half])
        k_hi, k_lo = khi_ref[c], klo_ref[c]
        nt = functools.partial(lax.dot_general, dimension_numbers=_NT, preferred_element_type=F32)
        s = nt(k_hi, q_hi) + nt(k_hi, q_lo) + nt(k_lo, q_hi)
        tops.append(_topk_rows(s, topk))
    (v0, i0), (v1, i1) = tops
    tb = v0.shape[1]
    sub8 = lax.broadcasted_iota(jnp.int32, (8, tb), 0).astype(F32)
    blocks, positions = [], []
    for a0, b in [(0, 0), (8, 0)] + [(0, b) for b in range(1, 8)]:
        blocks.append(v0[a0:a0 + 8, :] + v1[b:b + 1, :])
        positions.append((sub8 + float(a0)) * float(topk) + float(b))
    blocks.append(v0[0:1, :] + v1[8:16, :])
    positions.append(sub8 + 8.0)
    cand = jnp.stack(blocks, axis=0)
    pos = jnp.stack(positions, axis=0)
    a_iota = lax.broadcasted_iota(jnp.int32, (topk, tb), 0).astype(F32)
    best, sel_i, sel_j = [], [], []
    for _ in range(topk):
        m = jnp.max(jnp.max(cand, axis=0), axis=0, keepdims=True)
        p = jnp.min(jnp.min(jnp.where(cand == m[None], pos, float(topk * topk)), axis=0),
                    axis=0, keepdims=True)
        cand = jnp.where(pos == p[None], -jnp.inf, cand)
        pa = jnp.floor(p * (1.0 / topk))
        pb = p - pa * topk
        best.append(m)
        sel_i.append(jnp.sum(jnp.where(a_iota == pa, i0, 0.0), axis=0, keepdims=True))
        sel_j.append(jnp.sum(jnp.where(a_iota == pb, i1, 0.0), axis=0, keepdims=True))
    best = jnp.concatenate(best, axis=0)
    e = jnp.exp(best - best[0:1, :])
    g_ref[...] = e / jnp.sum(e, axis=0, keepdims=True)
    i_ref[...] = jnp.concatenate(sel_i, axis=0).astype(jnp.int32)
    j_ref[...] = jnp.concatenate(sel_j, axis=0).astype(jnp.int32)


def _route(h, wq_hi, wq_lo, keys_hi, keys_lo, tb):
    t, d = h.shape
    heads, _, n_keys, half = keys_hi.shape
    topk = PEER_TOPK
    assert topk == 16
    out_spec = pl.BlockSpec((topk, tb), lambda i, hh: (hh, i))
    key_spec = pl.BlockSpec((None, 2, n_keys, half), lambda i, hh: (hh, 0, 0, 0))
    w_spec = pl.BlockSpec((d, 2 * half), lambda i, hh: (0, hh))
    return pl.pallas_call(
        functools.partial(_route_kernel, half=half, topk=topk),
        grid=(t // tb, heads),
        in_specs=[pl.BlockSpec((tb, d), lambda i, hh: (i, 0)), w_spec, w_spec, key_spec, key_spec],
        out_specs=[out_spec] * 3,
        out_shape=[jax.ShapeDtypeStruct((heads * topk, t), jnp.int32)] * 2
                  + [jax.ShapeDtypeStruct((heads * topk, t), F32)],
        compiler_params=_params("arbitrary", "arbitrary"),
        name="peer_route",
    )(h, wq_hi, wq_lo, keys_hi, keys_lo)


def _gelu(x):
    return 0.5 * x * (1.0 + lax.erf(x * (1.0 / math.sqrt(2.0))))


def _peer_kernel(h_ref, i_ref, j_ref, gt_ref, u_ref, v_ref, x_ref, gate_ref, gf_ref, sh_ref, sc_ref,
                 y_ref, gmat, hb_scr, wd_scr, acc_ref, *, tb, n_keys, eb):
    e = pl.program_id(1)
    per = PEER_CHUNK // n_keys

    @pl.when(e == 0)
    def _():
        hb_scr[...] = h_ref[...].astype(BF16)
        acc_ref[...] = jnp.zeros_like(acc_ref)
        sub = lax.broadcasted_iota(jnp.int32, (n_keys, n_keys), 0)

        def build_one(t):
            irow = i_ref[pl.ds(t, 1), :]
            jrow = j_ref[pl.ds(t, 1), :]
            g = gt_ref[pl.ds(t, 1), :]
            g_hi = g.astype(BF16).astype(F32)
            hit_i = sub == irow
            a = jnp.concatenate([jnp.where(hit_i, g_hi, 0.0), jnp.where(hit_i, g - g_hi, 0.0)],
                                axis=1).astype(BF16)
            b = jnp.where(sub == jrow, 1.0, 0.0).astype(BF16)
            return lax.dot_general(a, jnp.concatenate([b, b], axis=1), _NT,
                                   preferred_element_type=F32)

        def build(o, carry):
            un = PEER_BUILD_UNROLL
            g = jnp.stack([build_one(o * un + k) for k in range(un)], axis=0)
            gmat[:, pl.ds(pl.multiple_of(o * un, un), un), :] = jnp.swapaxes(g, 0, 1)
            return carry

        lax.fori_loop(0, tb // PEER_BUILD_UNROLL, build, 0)

    hb = hb_scr[...]
    for cix in range(eb // PEER_CHUNK):
        lo = cix * PEER_CHUNK
        s = lax.dot_general(hb, u_ref[lo:lo + PEER_CHUNK, :], _NT, preferred_element_type=F32)
        first = e * (eb // n_keys) + cix * per
        gate = jnp.concatenate(
            [gmat[first + r] for r in range(per)], axis=1)
        wd_scr[:, lo:lo + PEER_CHUNK] = (gate * _gelu(s)).astype(BF16)
    acc_ref[...] += jnp.dot(wd_scr[...], v_ref[...], preferred_element_type=F32)

    @pl.when(e == pl.num_programs(1) - 1)
    def _():
        x3 = x_ref[...] + gate_ref[...] * acc_ref[...]
        y_ref[...] = _norm_mod(x3, gf_ref[...], sh_ref[...], sc_ref[...])


def _peer(h, i_sel, j_sel, g_sel, u_bf, v_bf, x, mods, fmods, gf, tb, blocks_per_batch):
    t, d = h.shape
    n_exp = u_bf.shape[0]
    n_keys = PEER_N_KEYS
    eb = PEER_EXPERT_BLOCK
    npick = i_sel.shape[1]
    nb, r, _ = mods.shape
    assert tb % PEER_BUILD_UNROLL == 0 and eb % PEER_CHUNK == 0 and PEER_CHUNK % n_keys == 0
    bmap = lambda i: i // blocks_per_batch
    tok = lambda width: pl.BlockSpec((tb, width), lambda i, e: (i, 0))
    tab = pl.BlockSpec((eb, d), lambda i, e: (e, 0))
    return pl.pallas_call(
        functools.partial(_peer_kernel, tb=tb, n_keys=n_keys, eb=eb),
        grid=(t // tb, n_exp // eb),
        in_specs=[tok(d), tok(npick), tok(npick), tok(npick), tab, tab, tok(d),
                  pl.BlockSpec((None, r, d), lambda i, e: (bmap(i), 0, 5)),
                  pl.BlockSpec((1, d), lambda i, e: (0, 0)),
                  pl.BlockSpec((None, r, d), lambda i, e: (bmap(i), 0, 0)),
                  pl.BlockSpec((None, r, d), lambda i, e: (bmap(i), 0, 1))],
        out_specs=tok(d),
        out_shape=jax.ShapeDtypeStruct((t, d), F32),
        scratch_shapes=[pltpu.VMEM((n_keys, tb, n_keys), F32), pltpu.VMEM((tb, d), BF16),
                        pltpu.VMEM((tb, eb), BF16), pltpu.VMEM((tb, d), F32)],
        compiler_params=_params("arbitrary", "arbitrary"),
        name="peer_mix",
    )(h, i_sel, j_sel, g_sel, u_bf, v_bf, x, mods, gf.reshape(1, d), fmods, fmods)


def _token_path(x, mods, fmods, pos, tb, blocks_per_batch, attend, w):
    proj = _inproj(x, w["norm_mix_g"], mods, w["w_in"], pos, tb, blocks_per_batch)
    sbo, do = attend(*proj)
    x2, h2 = _merge(sbo, do, x, w["w_out"], w["diff_norm_g"], w["norm_ffn_g"], mods,
                    w["lam_init"], w["heads"], tb, blocks_per_batch)
    i_sel, j_sel, g_sel = _route(h2, w["wq_hi"], w["wq_lo"], w["keys_hi"], w["keys_lo"], tb)
    y = _peer(h2, i_sel.T, j_sel.T, g_sel.T, w["u"], w["v"], x2, mods, fmods,
              w["norm_final_g"], tb, blocks_per_batch)
    return y, proj


def kernel(x_prompt, x_sample, c_prompt, c_sample, cache_sb_k, cache_sb_v, cache_diff_k, cache_diff_v, page_table, norm_mix_g, norm_ffn_g, w_ada, b_ada, w_in, diff_lambda, diff_norm_g, w_out, peer_w_q, peer_sub_keys, peer_u, peer_v, norm_final_g, w_ada_final, b_ada_final):
    batch, seq, d = x_prompt.shape
    nsamp, dec_seq, _ = x_sample.shape
    depth = w_ada.shape[0]
    assert depth == 1 and dec_seq == 1
    heads = cache_sb_k.shape[3]
    hd = cache_sb_k.shape[4]
    past_len = page_table.shape[1] * cache_sb_k.shape[2]
    l = 0
    lam_init = 0.8 - 0.6 * math.exp(-0.3 * l)

    nc = batch + nsamp
    c_all = jnp.concatenate([c_prompt, c_sample, jnp.zeros((-nc % 8, d), F32)], axis=0)
    mods = _adaln(c_all, w_ada[l], b_ada[l])
    fmods = _adaln(c_all, w_ada_final, b_ada_final)
    mods_p, mods_s = mods[:batch, None, :], mods[None, batch:nc, :]
    fmods_p, fmods_s = fmods[:batch, None, :], fmods[None, batch:nc, :]

    wq_hi, wq_lo = _split_bf16(peer_w_q[l])
    keys_hi, keys_lo = _split_bf16(peer_sub_keys[l])
    w = dict(norm_mix_g=norm_mix_g[l], norm_ffn_g=norm_ffn_g[l], w_in=w_in[l].astype(BF16),
             w_out=w_out[l].astype(BF16), diff_norm_g=diff_norm_g[l], lam_init=lam_init, heads=heads,
             wq_hi=wq_hi, wq_lo=wq_lo, keys_hi=keys_hi, keys_lo=keys_lo,
             u=peer_u[l].astype(BF16), v=peer_v[l].astype(BF16), norm_final_g=norm_final_g)
    lam_p = diff_lambda[l]

    def attend_prompt(sb_q, sb_k, sb_v, d_q, d_k, d_v):
        return (_sb_prompt(sb_q, sb_k, sb_v, batch, heads, ATTN_BLOCK, SB_CHUNK),
                _diff_prompt(d_q, d_k, d_v, lam_p, lam_init, batch, heads, ATTN_BLOCK))

    def attend_sample(sb_q, sb_k, sb_v, d_q, d_k, d_v):
        pool, page = cache_sb_k.shape[1], cache_sb_k.shape[2]
        rows = lambda c: c.reshape(pool, page * heads, hd)
        dkt = jnp.transpose(cache_diff_k, (0, 1, 3, 4, 5, 2)).reshape(pool, heads * hd, page)
        return _sample_attn(sb_q, d_q, d_k, d_v, rows(cache_sb_k), rows(cache_sb_v), dkt,
                            rows(cache_diff_v), page_table, lam_p, lam_init, heads,
                            math.gcd(PAGES_PER_STEP, page_table.shape[1]))

    tb_p = TOKEN_BLOCK
    y_p, proj_p = _token_path(x_prompt.reshape(batch * seq, d), mods_p, fmods_p, jnp.arange(seq),
                              tb_p, seq // tb_p, attend_prompt, w)
    y_s, proj_s = _token_path(x_sample.reshape(nsamp, d), mods_s, fmods_s,
                              jnp.full((nsamp,), past_len, jnp.int32), nsamp, 1, attend_sample, w)

    def rows(proj, n, t):
        _, sb_k, sb_v, _, d_k, d_v = proj
        return (sb_k.reshape(1, n, t, heads, hd), sb_v.reshape(1, n, t, heads, hd),
                d_k.reshape(1, n, t, heads, 2, hd // 2), d_v.reshape(1, n, t, heads, hd))

    return ((y_p.reshape(batch, seq, d), y_s.reshape(nsamp, dec_seq, d))
            + rows(proj_p, batch, seq) + rows(proj_s, nsamp, dec_seq))
```

```python
import functools
import math

import jax
import jax.numpy as jnp
import numpy as np
from jax import lax
from jax.experimental import pallas as pl
from jax.experimental.pallas import tpu as pltpu

F32 = jnp.float32
BF16 = jnp.bfloat16

EPS = 1e-6
ROPE_THETA = 500000.0
PEER_N_KEYS = 128
PEER_TOPK = 16
PEER_HEADS = 8
LANES = 128
VMEM_LIMIT = 56 * 1024 * 1024
TOKEN_BLOCK = 256
ATTN_BLOCK = 512
SB_CHUNK = 256
PAGES_PER_STEP = 4
INPROJ_TOKEN_BLOCK = 512
PEER_TOKEN_BLOCK = 512
PEER_EXPERT_BLOCK = 512
PEER_CHUNK = 256
PEER_BUILD_UNROLL = 16

_NT = (((1,), (1,)), ((), ()))


def _params(*sem):
    return pltpu.CompilerParams(dimension_semantics=sem, vmem_limit_bytes=VMEM_LIMIT)


def _split_bf16(x):
    hi = x.astype(BF16)
    lo = (x - hi.astype(F32)).astype(BF16)
    return hi, lo


def _norm_mod(x, g, shift, scale):
    ms = jnp.mean(x * x, axis=-1, keepdims=True)
    y = x * lax.rsqrt(ms + EPS) * g
    return y * (1.0 + scale) + shift


def _adaln_kernel(c_ref, w_ref, b_ref, o_ref):
    c = c_ref[...]
    a = (c * jax.nn.sigmoid(c)).astype(BF16)
    o_ref[...] = jnp.dot(a, w_ref[...].astype(BF16), preferred_element_type=F32) + b_ref[...]


def _adaln(c, w, b, tn=1024):
    m, d = c.shape
    n = w.shape[1]
    return pl.pallas_call(
        _adaln_kernel,
        grid=(n // tn,),
        in_specs=[pl.BlockSpec((m, d), lambda j: (0, 0)),
                  pl.BlockSpec((d, tn), lambda j: (0, j)),
                  pl.BlockSpec((1, tn), lambda j: (0, j))],
        out_specs=pl.BlockSpec((m, tn), lambda j: (0, j)),
        out_shape=jax.ShapeDtypeStruct((m, n), F32),
        compiler_params=_params("arbitrary"),
        name="adaln",
    )(c, w, b.reshape(1, n))


def _rope_tables(pos):
    rot = 16
    lane = np.arange(LANES)
    sub = lane % 64
    inv = ROPE_THETA ** (-jnp.arange(0, rot, 2, dtype=F32) / rot)
    ang = pos.astype(F32)[:, None] * inv
    ang = ang[:, sub % 8]
    cos, sin = jnp.cos(ang), jnp.sin(ang)
    first = jnp.asarray(sub < 8)
    second = jnp.asarray((sub >= 8) & (sub < 16))
    c = jnp.where(first | second, cos, 1.0)
    s_lo = jnp.where(first, -sin, 0.0)
    s_hi = jnp.where(second, sin, 0.0)
    return c, s_lo, s_hi


def _prenorm_kernel(x_ref, g_ref, sh_ref, sc_ref, h_ref):
    h_ref[...] = _norm_mod(x_ref[...], g_ref[...], sh_ref[...], sc_ref[...]).astype(BF16)


def _prenorm(x, g, mods, tb, blocks_per_batch):
    t, d = x.shape
    nb, r, _ = mods.shape
    bmap = lambda i: i // blocks_per_batch
    mod_spec = lambda k: pl.BlockSpec((None, r, d), lambda i: (bmap(i), 0, k))
    tok = pl.BlockSpec((tb, d), lambda i: (i, 0))
    return pl.pallas_call(
        _prenorm_kernel,
        grid=(t // tb,),
        in_specs=[tok, pl.BlockSpec((1, d), lambda i: (0, 0)), mod_spec(0), mod_spec(1)],
        out_specs=tok,
        out_shape=jax.ShapeDtypeStruct((t, d), BF16),
        compiler_params=_params("arbitrary"),
        name="prenorm",
    )(x, g.reshape(1, d), mods, mods)


def _inproj_kernel(h_ref, w_ref, cos_ref, slo_ref, shi_ref, o_ref, w_scr, *, rope_blocks):
    n = pl.program_id(0)

    @pl.when(pl.program_id(1) == 0)
    def _():
        w_scr[...] = w_ref[...].astype(BF16)

    rope = functools.reduce(jnp.logical_or, [n == b for b in rope_blocks])

    @pl.when(jnp.logical_not(rope))
    def _():
        o_ref[...] = jnp.dot(h_ref[...], w_scr[...], preferred_element_type=F32)

    @pl.when(rope)
    def _():
        cw = 2 * LANES
        cos, slo, shi = (jnp.tile(r[...], (1, cw // LANES)) for r in (cos_ref, slo_ref, shi_ref))
        for j in range(w_scr.shape[1] // cw):
            zc = jnp.dot(h_ref[...], w_scr[:, j * cw:(j + 1) * cw], preferred_element_type=F32)
            o_ref[:, j * cw:(j + 1) * cw] = (
                zc * cos + pltpu.roll(zc, cw - 8, 1) * slo + pltpu.roll(zc, 8, 1) * shi)


def _inproj(h, w, pos, tb):
    t, d = h.shape
    wn = 1024
    nblk = w.shape[1] // wn
    assert nblk == 6
    cos, slo, shi = _rope_tables(pos)
    npos = pos.shape[0] // tb
    tab_spec = pl.BlockSpec((tb, LANES), lambda n, i: (i % npos, 0))
    return pl.pallas_call(
        functools.partial(_inproj_kernel, rope_blocks=(3, 4)),
        grid=(nblk, t // tb),
        in_specs=[pl.BlockSpec((tb, d), lambda n, i: (i, 0)),
                  pl.BlockSpec((d, wn), lambda n, i: (0, n)),
                  tab_spec, tab_spec, tab_spec],
        out_specs=pl.BlockSpec((None, tb, wn), lambda n, i: (n, i, 0)),
        out_shape=jax.ShapeDtypeStruct((nblk, t, wn), F32),
        scratch_shapes=[pltpu.VMEM((d, wn), BF16)],
        compiler_params=_params("arbitrary", "arbitrary"),
        name="inproj",
    )(h, w, cos, slo, shi)


def _softplus(z):
    return jnp.maximum(z, 0.0) + jnp.log(1.0 + jnp.exp(-jnp.abs(z)))


def _suffix_matrix(n):
    s = lax.broadcasted_iota(jnp.int32, (n, n), 0)
    j = lax.broadcasted_iota(jnp.int32, (n, n), 1)
    return jnp.where(s > j, 1.0, 0.0).astype(BF16)


def _lane_tile(x, width):
    return jnp.tile(x, (1, width // LANES))


def _sb_chunk(z, v_bf, c, mask):
    sp = _softplus(z)
    lk = -sp if mask is None else jnp.where(mask, -sp, 0.0)
    lk_hi, lk_lo = _split_bf16(lk)
    u = _suffix_matrix(z.shape[1])
    after = (jnp.dot(lk_hi, u, preferred_element_type=F32)
             + jnp.dot(lk_lo, u, preferred_element_type=F32))
    a = jnp.exp((z - sp) + (after + _lane_tile(c, z.shape[1])))
    if mask is not None:
        a = jnp.where(mask, a, 0.0)
    contrib = jnp.dot(a.astype(BF16), v_bf, preferred_element_type=F32)
    return contrib, c + (after[:, 0:1] + lk[:, 0:1])


def _sb_prompt_kernel(qb_ref, kb_ref, q_ref, k_ref, v_ref, o_ref, acc_ref, c_ref, *, tq, sub, scale):
    p = pl.program_id(2)
    qb, kb = qb_ref[p], kb_ref[p]

    def step(diagonal):
        q = q_ref[...].astype(BF16)
        if diagonal:
            acc, c = jnp.zeros(acc_ref.shape, F32), jnp.zeros(c_ref.shape, F32)
            row = lax.broadcasted_iota(jnp.int32, (tq, sub), 0)
            col = lax.broadcasted_iota(jnp.int32, (tq, sub), 1)
        else:
            acc, c = acc_ref[...], c_ref[...]
        for s in reversed(range(tq // sub)):
            k = k_ref[s * sub:(s + 1) * sub, :].astype(BF16)
            v = v_ref[s * sub:(s + 1) * sub, :].astype(BF16)
            z = lax.dot_general(q, k, _NT, preferred_element_type=F32) * scale
            mask = (col + s * sub) < row if diagonal else None
            contrib, c = _sb_chunk(z, v, c, mask)
            acc = acc + contrib
        acc_ref[...] = acc
        c_ref[...] = c

    pl.when(kb == qb)(lambda: step(True))
    pl.when(kb != qb)(lambda: step(False))

    @pl.when(kb == 0)
    def _():
        o_ref[...] = acc_ref[...]


def _tri_pairs(nq):
    qs, ks = [], []
    for q in range(nq):
        for k in range(q, -1, -1):
            qs.append(q)
            ks.append(k)
    return jnp.asarray(qs, jnp.int32), jnp.asarray(ks, jnp.int32)


def _qkv_specs(first, nq, tq, hd):
    qspec = pl.BlockSpec((None, tq, hd), lambda b, h, p, qa, ka: (first, b * nq + qa[p], h))
    kspec = pl.BlockSpec((None, tq, hd), lambda b, h, p, qa, ka: (first + 1, b * nq + ka[p], h))
    vspec = pl.BlockSpec((None, tq, hd), lambda b, h, p, qa, ka: (first + 2, b * nq + ka[p], h))
    return [qspec, kspec, vspec]


def _sb_prompt(proj, batch, heads, tq, sub):
    _, t, w = proj.shape
    hd = w // heads
    nq = t // batch // tq
    qs, ks = _tri_pairs(nq)
    return pl.pallas_call(
        functools.partial(_sb_prompt_kernel, tq=tq, sub=sub, scale=1.0 / math.sqrt(hd)),
        grid_spec=pltpu.PrefetchScalarGridSpec(
            num_scalar_prefetch=2,
            grid=(batch, heads, qs.shape[0]),
            in_specs=_qkv_specs(0, nq, tq, hd),
            out_specs=pl.BlockSpec((tq, hd), lambda b, h, p, qa, ka: (b * nq + qa[p], h)),
            scratch_shapes=[pltpu.VMEM((tq, hd), F32), pltpu.VMEM((tq, LANES), F32)]),
        out_shape=jax.ShapeDtypeStruct((t, w), F32),
        compiler_params=_params("arbitrary", "arbitrary", "arbitrary"),
        name="sb_prompt",
    )(qs, ks, proj, proj, proj)


def _lambda(lp, lam_init):
    a = jnp.sum(lp[0:1, :] * lp[1:2, :], axis=1, keepdims=True)
    b = jnp.sum(lp[2:3, :] * lp[3:4, :], axis=1, keepdims=True)
    return jnp.exp(a) - jnp.exp(b) + lam_init


def _diff_prompt_kernel(qb_ref, kb_ref, q_ref, k_ref, v_ref, lam_ref, o_ref,
                        q2_scr, acc_ref, m_ref, l_ref, *, tq, dqk, scale, lam_init):
    p = pl.program_id(2)
    qb, kb = qb_ref[p], kb_ref[p]
    fold = math.frexp(scale)[0] == 0.5

    def step(diagonal):
        if diagonal:
            q = q_ref[...] * scale if fold else q_ref[...]
            lane = lax.broadcasted_iota(jnp.int32, q.shape, 1)
            q2 = jnp.concatenate([jnp.where(lane < dqk, q, 0.0), jnp.where(lane >= dqk, q, 0.0)],
                                 axis=0).astype(BF16)
            q2_scr[...] = q2
        else:
            q2 = q2_scr[...]
        s = lax.dot_general(q2, k_ref[...].astype(BF16), _NT, preferred_element_type=F32)
        if not fold:
            s = s * scale
        v = v_ref[...].astype(BF16)
        if diagonal:
            row = lax.broadcasted_iota(jnp.int32, s.shape, 0)
            col = lax.broadcasted_iota(jnp.int32, s.shape, 1)
            s = jnp.where(col <= jnp.where(row >= tq, row - tq, row), s, -jnp.inf)
            m_new = jnp.broadcast_to(jnp.max(s, axis=1, keepdims=True), m_ref.shape)
            pr = jnp.exp(s - _lane_tile(m_new, tq))
            l_ref[...] = jnp.broadcast_to(jnp.sum(pr, axis=1, keepdims=True), l_ref.shape)
            acc_ref[...] = jnp.dot(pr.astype(BF16), v, preferred_element_type=F32)
        else:
            m_old = m_ref[...]
            m_new = jnp.maximum(m_old, jnp.max(s, axis=1, keepdims=True))
            alpha = jnp.exp(m_old - m_new)
            pr = jnp.exp(s - _lane_tile(m_new, tq))
            l_ref[...] = alpha * l_ref[...] + jnp.sum(pr, axis=1, keepdims=True)
            acc_ref[...] = alpha * acc_ref[...] + jnp.dot(pr.astype(BF16), v, preferred_element_type=F32)
        m_ref[...] = m_new

    pl.when(kb == qb)(lambda: step(True))
    pl.when(kb != qb)(lambda: step(False))

    @pl.when(kb == 0)
    def _():
        lam = _lambda(lam_ref[...], lam_init)
        nrm = acc_ref[...] / l_ref[...]
        o_ref[...] = nrm[:tq] - lam * nrm[tq:]


def _diff_prompt(proj, lam_p, lam_init, batch, heads, tq):
    _, t, w = proj.shape
    hd = w // heads
    dqk = hd // 2
    nq = t // batch // tq
    qs, ks = _tri_pairs(nq)
    return pl.pallas_call(
        functools.partial(_diff_prompt_kernel, tq=tq, dqk=dqk, scale=1.0 / math.sqrt(dqk),
                          lam_init=lam_init),
        grid_spec=pltpu.PrefetchScalarGridSpec(
            num_scalar_prefetch=2,
            grid=(batch, heads, qs.shape[0]),
            in_specs=_qkv_specs(3, nq, tq, hd)
                     + [pl.BlockSpec(lam_p.shape, lambda b, h, p, qa, ka: (0, 0))],
            out_specs=pl.BlockSpec((tq, hd), lambda b, h, p, qa, ka: (b * nq + qa[p], h)),
            scratch_shapes=[pltpu.VMEM((2 * tq, hd), BF16), pltpu.VMEM((2 * tq, hd), F32),
                            pltpu.VMEM((2 * tq, LANES), F32), pltpu.VMEM((2 * tq, LANES), F32)]),
        out_shape=jax.ShapeDtypeStruct((t, w), F32),
        compiler_params=_params("arbitrary", "arbitrary", "arbitrary"),
        name="diff_prompt",
    )(qs, ks, proj, proj, proj, lam_p)


def _sample_attn_kernel(pt_ref, sbq_ref, dq_ref, dkn_ref, dvn_ref, *rest, heads, hd, lam_init, pps):
    ksb_refs, vsb_refs, kdt_refs, vd_refs = (rest[i * pps:(i + 1) * pps] for i in range(4))
    (lam_ref, osb_ref, od_ref, qsb_scr, qd_scr, acc_sb, c_scr, acc_d, m_scr, l_scr) = rest[4 * pps:]
    j = pl.program_id(1)
    w = heads * hd
    dqk = hd // 2
    rows = 2 * heads
    page = kdt_refs[0].shape[1]
    rr = lax.broadcasted_iota(jnp.int32, (rows, hd), 0)

    @pl.when(j == 0)
    def _():
        r_i = lax.broadcasted_iota(jnp.int32, (rows, w), 0)
        l_i = lax.broadcasted_iota(jnp.int32, (rows, w), 1)
        d_mask = (r_i % heads == l_i // hd) & (r_i // heads == (l_i // dqk) % 2)
        qd = jnp.where(d_mask, dq_ref[...], 0.0).astype(BF16)
        qd_scr[...] = qd
        qsb_scr[...] = jnp.concatenate([sbq_ref[...], jnp.zeros((heads, hd), F32)], axis=0).astype(BF16)
        acc_sb[...] = jnp.zeros_like(acc_sb)
        c_scr[...] = jnp.zeros_like(c_scr)
        kn = dkn_ref[...].astype(BF16).astype(F32)
        vn = dvn_ref[...].astype(BF16).astype(F32)
        m_scr[...] = jnp.sum(qd.astype(F32) * kn, axis=1, keepdims=True) * (1.0 / math.sqrt(dqk))
        l_scr[...] = jnp.ones_like(l_scr)
        acc_d[...] = jnp.concatenate([vn, vn], axis=0)

    def head_rows(ref, h):
        return ref[pl.ds(h, page, stride=heads), :].astype(BF16)

    def head_rows_all(refs, h):
        return jnp.concatenate([head_rows(r, h) for r in refs], axis=0)

    qsb = qsb_scr[...]
    rw = lax.broadcasted_iota(jnp.int32, (rows, pps * page), 0)

    z = jnp.zeros((rows, pps * page), F32)
    for h in range(heads):
        zh = lax.dot_general(qsb, head_rows_all(ksb_refs, h), _NT, preferred_element_type=F32)
        z = jnp.where(rw == h, zh, z)
    z = z * (1.0 / math.sqrt(hd))
    sp = _softplus(z)
    ls = z - sp
    chunks = [-sp[:, p * page:(p + 1) * page] for p in range(pps)]
    parts = [x for lk in chunks for x in _split_bf16(lk)]
    within = jnp.dot(jnp.concatenate(parts, axis=0), _suffix_matrix(page), preferred_element_type=F32)
    c = c_scr[...]
    a_parts = []
    for p in range(pps):
        after = within[2 * p * rows:(2 * p + 1) * rows] + within[(2 * p + 1) * rows:(2 * p + 2) * rows]
        a_parts.append(jnp.exp(ls[:, p * page:(p + 1) * page] + (after + c)))
        c = c + after[:, 0:1] + chunks[p][:, 0:1]
    c_scr[...] = c
    a = jnp.concatenate(a_parts, axis=1).astype(BF16)
    asb = acc_sb[...]
    for h in range(heads):
        oh = jnp.dot(a, head_rows_all(vsb_refs, h), preferred_element_type=F32)
        asb = asb + jnp.where(rr == h, oh, 0.0)
    acc_sb[...] = asb

    kdt = jnp.concatenate([r[...].astype(BF16) for r in kdt_refs], axis=1)
    s = jnp.dot(qd_scr[...], kdt, preferred_element_type=F32) * (1.0 / math.sqrt(dqk))
    m = m_scr[...]
    m_new = jnp.maximum(m, jnp.max(s, axis=1, keepdims=True))
    alpha = jnp.exp(m - m_new)
    pr = jnp.exp(s - m_new)
    l = alpha * l_scr[...] + jnp.sum(pr, axis=1, keepdims=True)
    pr = pr.astype(BF16)
    ad = alpha * acc_d[...]
    for h in range(heads):
        oh = jnp.dot(pr, head_rows_all(vd_refs, h), preferred_element_type=F32)
        ad = ad + jnp.where(rr % heads == h, oh, 0.0)
    m_scr[...], l_scr[...], acc_d[...] = m_new, l, ad

    @pl.when(j == pl.num_programs(1) - 1)
    def _():
        osb_ref[...] = asb[:heads, :]
        lam = _lambda(lam_ref[...], lam_init)
        nrm = ad / l
        od_ref[...] = nrm[:heads, :] - lam * nrm[heads:, :]


def _sample_attn(sbq, dq, dkn, dvn, c_sbk, c_sbv, c_dkt, c_dv, page_table, lam_p, lam_init, heads, pps):
    n, w = sbq.shape
    hd = w // heads
    npages = page_table.shape[1]
    page = c_dkt.shape[2]
    rows = 2 * heads
    assert npages % pps == 0
    row_spec = pl.BlockSpec((None, 1, w), lambda b, j, pt: (b, 0, 0))
    head_spec = pl.BlockSpec((None, heads, hd), lambda b, j, pt: (b, 0, 0))

    def cache_specs(shape):
        return [pl.BlockSpec((None,) + shape, lambda b, j, pt, p=p: (pt[b, npages - 1 - (j * pps + p)], 0, 0))
                for p in range(pps)]

    kv_specs = cache_specs((page * heads, hd))
    osb, od = pl.pallas_call(
        functools.partial(_sample_attn_kernel, heads=heads, hd=hd, lam_init=lam_init, pps=pps),
        grid_spec=pltpu.PrefetchScalarGridSpec(
            num_scalar_prefetch=1,
            grid=(n, npages // pps),
            in_specs=[head_spec, row_spec, row_spec, head_spec]
                     + kv_specs + kv_specs + cache_specs((w, page)) + kv_specs
                     + [pl.BlockSpec(lam_p.shape, lambda b, j, pt: (0, 0))],
            out_specs=[head_spec, head_spec],
            scratch_shapes=[pltpu.VMEM((rows, hd), BF16), pltpu.VMEM((rows, w), BF16),
                            pltpu.VMEM((rows, hd), F32), pltpu.VMEM((rows, 1), F32),
                            pltpu.VMEM((rows, hd), F32), pltpu.VMEM((rows, 1), F32),
                            pltpu.VMEM((rows, 1), F32)]),
        out_shape=[jax.ShapeDtypeStruct((n, heads, hd), F32)] * 2,
        compiler_params=_params("arbitrary", "arbitrary"),
        name="sample_attn",
    )(page_table, sbq.reshape(n, heads, hd), dq.reshape(n, 1, w), dkn.reshape(n, 1, w),
      dvn.reshape(n, heads, hd), *([c_sbk] * pps), *([c_sbv] * pps), *([c_dkt] * pps), *([c_dv] * pps),
      lam_p)
    return osb.reshape(n, w), od.reshape(n, w)


def _merge_kernel(sbo_ref, do_ref, x_ref, w_ref, dg_ref, gate_ref, g2_ref, sh_ref, sc_ref,
                  x2_ref, h2_ref, h2b_ref, o_scr, *, heads, hd, lam_init):
    sbw = heads * hd
    o_scr[:, :sbw] = sbo_ref[...].astype(BF16)
    dg = dg_ref[...]
    for h in range(heads):
        d = do_ref[:, h * hd:(h + 1) * hd]
        ms = jnp.mean(d * d, axis=-1, keepdims=True)
        dn = (d * lax.rsqrt(ms + EPS) * dg) * (1.0 - lam_init)
        o_scr[:, sbw + h * hd:sbw + (h + 1) * hd] = dn.astype(BF16)
    o = jnp.dot(o_scr[...], w_ref[...], preferred_element_type=F32)
    x2 = x_ref[...] + gate_ref[...] * o
    x2_ref[...] = x2
    h2 = _norm_mod(x2, g2_ref[...], sh_ref[...], sc_ref[...])
    h2_ref[...] = h2
    h2b_ref[...] = h2.astype(BF16)


def _merge(sbo, do, x, w_bf, dg, g2, mods, lam_init, heads, tb, blocks_per_batch):
    t, d = x.shape
    nb, r, _ = mods.shape
    w = sbo.shape[1]
    hd = w // heads
    bmap = lambda i: i // blocks_per_batch
    mod_spec = lambda k: pl.BlockSpec((None, r, d), lambda i: (bmap(i), 0, k))
    tok = lambda width: pl.BlockSpec((tb, width), lambda i: (i, 0))
    return pl.pallas_call(
        functools.partial(_merge_kernel, heads=heads, hd=hd, lam_init=lam_init),
        grid=(t // tb,),
        in_specs=[tok(w), tok(w), tok(d),
                  pl.BlockSpec(w_bf.shape, lambda i: (0, 0)),
                  pl.BlockSpec((1, hd), lambda i: (0, 0)),
                  mod_spec(2),
                  pl.BlockSpec((1, d), lambda i: (0, 0)),
                  mod_spec(3), mod_spec(4)],
        out_specs=[tok(d), tok(d), tok(d)],
        out_shape=[jax.ShapeDtypeStruct((t, d), F32)] * 2 + [jax.ShapeDtypeStruct((t, d), BF16)],
        scratch_shapes=[pltpu.VMEM((tb, 2 * w), BF16)],
        compiler_params=_params("arbitrary"),
        name="merge_outproj",
    )(sbo, do, x, w_bf, dg.reshape(1, hd), mods, g2.reshape(1, d), mods, mods)


def _topk_rows(s, k):
    n = s.shape[0]
    idx = lax.broadcasted_iota(jnp.int32, s.shape, 0).astype(F32)
    vals, ids = [], []
    for _ in range(k):
        m = jnp.max(s, axis=0, keepdims=True)
        i = jnp.min(jnp.where(s == m, idx, float(n)), axis=0, keepdims=True)
        s = jnp.where(idx == i, -jnp.inf, s)
        vals.append(m)
        ids.append(i)
    return jnp.concatenate(vals, axis=0), jnp.concatenate(ids, axis=0)


def _route_kernel(h_ref, whi_ref, wlo_ref, khi_ref, klo_ref, i_ref, j_ref, g_ref, *, half, topk):
    h_hi, h_lo = _split_bf16(h_ref[...])
    dot = functools.partial(jnp.dot, preferred_element_type=F32)
    q = dot(h_hi, whi_ref[...]) + dot(h_hi, wlo_ref[...]) + dot(h_lo, whi_ref[...])
    tops = []
    for c in range(2):
        q_hi, q_lo = _split_bf16(q[:, c * half:(c + 1) * half])
        k_hi, k_lo = khi_ref[c], klo_ref[c]
        nt = functools.partial(lax.dot_general, dimension_numbers=_NT, preferred_element_type=F32)
        s = nt(k_hi, q_hi) + nt(k_hi, q_lo) + nt(k_lo, q_hi)
        tops.append(_topk_rows(s, topk))
    (v0, i0), (v1, i1) = tops
    tb = v0.shape[1]
    sub8 = lax.broadcasted_iota(jnp.int32, (8, tb), 0).astype(F32)
    blocks, positions = [], []
    for a0, b in [(0, 0), (8, 0)] + [(0, b) for b in range(1, 8)]:
        blocks.append(v0[a0:a0 + 8, :] + v1[b:b + 1, :])
        positions.append((sub8 + float(a0)) * float(topk) + float(b))
    blocks.append(v0[0:1, :] + v1[8:16, :])
    positions.append(sub8 + 8.0)
    cand = jnp.stack(blocks, axis=0)
    pos = jnp.stack(positions, axis=0)
    a_iota = lax.broadcasted_iota(jnp.int32, (topk, tb), 0).astype(F32)
    best, sel_i, sel_j = [], [], []
    for _ in range(topk):
        m = jnp.max(jnp.max(cand, axis=0), axis=0, keepdims=True)
        p = jnp.min(jnp.min(jnp.where(cand == m[None], pos, float(topk * topk)), axis=0),
                    axis=0, keepdims=True)
        cand = jnp.where(pos == p[None], -jnp.inf, cand)
        pa = jnp.floor(p * (1.0 / topk))
        pb = p - pa * topk
        best.append(m)
        sel_i.append(jnp.sum(jnp.where(a_iota == pa, i0, 0.0), axis=0, keepdims=True))
        sel_j.append(jnp.sum(jnp.where(a_iota == pb, i1, 0.0), axis=0, keepdims=True))
    best = jnp.concatenate(best, axis=0)
    e = jnp.exp(best - best[0:1, :])
    g_ref[...] = e / jnp.sum(e, axis=0, keepdims=True)
    i_ref[...] = jnp.concatenate(sel_i, axis=0).astype(jnp.int32)
    j_ref[...] = jnp.concatenate(sel_j, axis=0).astype(jnp.int32)


def _route(h, wq_hi, wq_lo, keys_hi, keys_lo, tb):
    t, d = h.shape
    heads, _, n_keys, half = keys_hi.shape
    topk = PEER_TOPK
    assert topk == 16
    out_spec = pl.BlockSpec((topk, tb), lambda i, hh: (hh, i))
    key_spec = pl.BlockSpec((None, 2, n_keys, half), lambda i, hh: (hh, 0, 0, 0))
    w_spec = pl.BlockSpec((d, 2 * half), lambda i, hh: (0, hh))
    return pl.pallas_call(
        functools.partial(_route_kernel, half=half, topk=topk),
        grid=(t // tb, heads),
        in_specs=[pl.BlockSpec((tb, d), lambda i, hh: (i, 0)), w_spec, w_spec, key_spec, key_spec],
        out_specs=[out_spec] * 3,
        out_shape=[jax.ShapeDtypeStruct((heads * topk, t), jnp.int32)] * 2
                  + [jax.ShapeDtypeStruct((heads * topk, t), F32)],
        compiler_params=_params("arbitrary", "arbitrary"),
        name="peer_route",
    )(h, wq_hi, wq_lo, keys_hi, keys_lo)


def _gelu(x):
    return 0.5 * x * (1.0 + lax.erf(x * (1.0 / math.sqrt(2.0))))


def _peer_kernel(h_ref, i_ref, j_ref, gt_ref, u_ref, v_ref, x_ref, gate_ref, gf_ref, sh_ref, sc_ref,
                 y_ref, gmat, wd_scr, *, tb, n_keys, eb):
    e = pl.program_id(1)
    per = PEER_CHUNK // n_keys

    @pl.when(e == 0)
    def _():
        y_ref[...] = jnp.zeros_like(y_ref)
        sub = lax.broadcasted_iota(jnp.int32, (n_keys, n_keys), 0)

        def build_one(t):
            irow = i_ref[pl.ds(t, 1), :]
            jrow = j_ref[pl.ds(t, 1), :]
            g = gt_ref[pl.ds(t, 1), :]
            a = jnp.where(sub == irow, g, 0.0).astype(BF16)
            b = jnp.where(sub == jrow, 1.0, 0.0).astype(BF16)
            return lax.dot_general(a, b, _NT, preferred_element_type=F32)

        def build(o, carry):
            un = PEER_BUILD_UNROLL
            g = jnp.stack([build_one(o * un + k) for k in range(un)], axis=0)
            gmat[:, pl.ds(pl.multiple_of(o * un, un), un), :] = jnp.swapaxes(g, 0, 1).astype(BF16)
            return carry

        lax.fori_loop(0, tb // PEER_BUILD_UNROLL, build, 0)

    hb = h_ref[...]
    for cix in range(eb // PEER_CHUNK):
        lo = cix * PEER_CHUNK
        s = lax.dot_general(hb, u_ref[lo:lo + PEER_CHUNK, :], _NT, preferred_element_type=F32)
        first = e * (eb // n_keys) + cix * per
        gate = jnp.concatenate([gmat[first + r] for r in range(per)], axis=1).astype(F32)
        wd_scr[:, lo:lo + PEER_CHUNK] = (gate * _gelu(s)).astype(BF16)
    y_ref[...] += jnp.dot(wd_scr[...], v_ref[...], preferred_element_type=F32)

    @pl.when(e == pl.num_programs(1) - 1)
    def _():
        x3 = x_ref[...] + gate_ref[...] * y_ref[...]
        y_ref[...] = _norm_mod(x3, gf_ref[...], sh_ref[...], sc_ref[...])


def _peer(h, i_sel, j_sel, g_sel, u_bf, v_bf, x, mods, fmods, gf, tb, blocks_per_batch):
    t, d = h.shape
    n_exp = u_bf.shape[0]
    n_keys = PEER_N_KEYS
    eb = PEER_EXPERT_BLOCK
    npick = i_sel.shape[1]
    nb, r, _ = mods.shape
    assert tb % PEER_BUILD_UNROLL == 0 and eb % PEER_CHUNK == 0 and PEER_CHUNK % n_keys == 0
    bmap = lambda i: i // blocks_per_batch
    tok = lambda width: pl.BlockSpec((tb, width), lambda i, e: (i, 0))
    tab = pl.BlockSpec((eb, d), lambda i, e: (e, 0))
    return pl.pallas_call(
        functools.partial(_peer_kernel, tb=tb, n_keys=n_keys, eb=eb),
        grid=(t // tb, n_exp // eb),
        in_specs=[tok(d), tok(npick), tok(npick), tok(npick), tab, tab, tok(d),
                  pl.BlockSpec((None, r, d), lambda i, e: (bmap(i), 0, 5)),
                  pl.BlockSpec((1, d), lambda i, e: (0, 0)),
                  pl.BlockSpec((None, r, d), lambda i, e: (bmap(i), 0, 0)),
                  pl.BlockSpec((None, r, d), lambda i, e: (bmap(i), 0, 1))],
        out_specs=tok(d),
        out_shape=jax.ShapeDtypeStruct((t, d), F32),
        scratch_shapes=[pltpu.VMEM((n_keys, tb, n_keys), BF16), pltpu.VMEM((tb, eb), BF16)],
        compiler_params=_params("arbitrary", "arbitrary"),
        name="peer_mix",
    )(h, i_sel, j_sel, g_sel, u_bf, v_bf, x, mods, gf.reshape(1, d), fmods, fmods)


def _token_path(x, mods, fmods, pos, tb, blocks_per_batch, attend, w):
    t = x.shape[0]
    h = _prenorm(x, w["norm_mix_g"], mods, tb, blocks_per_batch)
    proj = _inproj(h, w["w_in"], pos, min(INPROJ_TOKEN_BLOCK, t))
    sbo, do = attend(proj)
    x2, h2, h2b = _merge(sbo, do, x, w["w_out"], w["diff_norm_g"], w["norm_ffn_g"], mods,
                    w["lam_init"], w["heads"], tb, blocks_per_batch)
    i_sel, j_sel, g_sel = _route(h2, w["wq_hi"], w["wq_lo"], w["keys_hi"], w["keys_lo"], tb)
    tb_peer = min(PEER_TOKEN_BLOCK, t)
    y = _peer(h2b, i_sel.T, j_sel.T, g_sel.T, w["u"], w["v"], x2, mods, fmods,
              w["norm_final_g"], tb_peer, blocks_per_batch * tb // tb_peer)
    return y, proj


def kernel(x_prompt, x_sample, c_prompt, c_sample, cache_sb_k, cache_sb_v, cache_diff_k, cache_diff_v, page_table, norm_mix_g, norm_ffn_g, w_ada, b_ada, w_in, diff_lambda, diff_norm_g, w_out, peer_w_q, peer_sub_keys, peer_u, peer_v, norm_final_g, w_ada_final, b_ada_final):
    batch, seq, d = x_prompt.shape
    nsamp, dec_seq, _ = x_sample.shape
    depth = w_ada.shape[0]
    assert depth == 1 and dec_seq == 1
    heads = cache_sb_k.shape[3]
    hd = cache_sb_k.shape[4]
    past_len = page_table.shape[1] * cache_sb_k.shape[2]
    l = 0
    lam_init = 0.8 - 0.6 * math.exp(-0.3 * l)

    nc = batch + nsamp
    c_all = jnp.concatenate([c_prompt, c_sample, jnp.zeros((-nc % 8, d), F32)], axis=0)
    mods = _adaln(c_all, w_ada[l], b_ada[l])
    fmods = _adaln(c_all, w_ada_final, b_ada_final)
    mods_p, mods_s = mods[:batch, None, :], mods[None, batch:nc, :]
    fmods_p, fmods_s = fmods[:batch, None, :], fmods[None, batch:nc, :]

    wq_hi, wq_lo = _split_bf16(peer_w_q[l])
    keys_hi, keys_lo = _split_bf16(peer_sub_keys[l])
    w = dict(norm_mix_g=norm_mix_g[l], norm_ffn_g=norm_ffn_g[l], w_in=w_in[l],
             w_out=w_out[l].astype(BF16), diff_norm_g=diff_norm_g[l], lam_init=lam_init, heads=heads,
             wq_hi=wq_hi, wq_lo=wq_lo, keys_hi=keys_hi, keys_lo=keys_lo,
             u=peer_u[l].astype(BF16), v=peer_v[l].astype(BF16), norm_final_g=norm_final_g)
    lam_p = diff_lambda[l]

    def attend_prompt(proj):
        return (_sb_prompt(proj, batch, heads, ATTN_BLOCK, SB_CHUNK),
                _diff_prompt(proj, lam_p, lam_init, batch, heads, ATTN_BLOCK))

    def attend_sample(proj):
        sb_q, _, _, d_q, d_k, d_v = proj
        pool, page = cache_sb_k.shape[1], cache_sb_k.shape[2]
        rows = lambda c: c.reshape(pool, page * heads, hd)
        dkt = jnp.transpose(cache_diff_k, (0, 1, 3, 4, 5, 2)).reshape(pool, heads * hd, page)
        return _sample_attn(sb_q, d_q, d_k, d_v, rows(cache_sb_k), rows(cache_sb_v), dkt,
                            rows(cache_diff_v), page_table, lam_p, lam_init, heads,
                            math.gcd(PAGES_PER_STEP, page_table.shape[1]))

    tb_p = TOKEN_BLOCK
    y_p, proj_p = _token_path(x_prompt.reshape(batch * seq, d), mods_p, fmods_p, jnp.arange(seq),
                              tb_p, seq // tb_p, attend_prompt, w)
    y_s, proj_s = _token_path(x_sample.reshape(nsamp, d), mods_s, fmods_s,
                              jnp.full((nsamp,), past_len, jnp.int32), nsamp, 1, attend_sample, w)

    def rows(proj, n, t):
        sb_k, sb_v, d_k, d_v = proj[1], proj[2], proj[4], proj[5]
        return (sb_k.reshape(1, n, t, heads, hd), sb_v.reshape(1, n, t, heads, hd),
                d_k.reshape(1, n, t, heads, 2, hd // 2), d_v.reshape(1, n, t, heads, hd))

    return ((y_p.reshape(batch, seq, d), y_s.reshape(nsamp, dec_seq, d))
            + rows(proj_p, batch, seq) + rows(proj_s, nsamp, dec_seq))
```

```python
import functools
import math

import jax
import jax.numpy as jnp
import numpy as np
from jax import lax
from jax.experimental import pallas as pl
from jax.experimental.pallas import tpu as pltpu

F32 = jnp.float32
BF16 = jnp.bfloat16

EPS = 1e-6
ROPE_THETA = 500000.0
PEER_N_KEYS = 128
PEER_TOPK = 16
PEER_HEADS = 8
LANES = 128
VMEM_LIMIT = 56 * 1024 * 1024
TOKEN_BLOCK = 256
ATTN_BLOCK = 512
SB_CHUNK = 256
PAGES_PER_STEP = 8
INPROJ_TOKEN_BLOCK = 512
PEER_TOKEN_BLOCK = 512
PEER_EXPERT_BLOCK = 512
PEER_CHUNK = 256
ROUTE_HEADS = 2
PEER_ROWS = 256
PEER_BUILD_UNROLL = 16

_NT = (((1,), (1,)), ((), ()))


def _params(*sem):
    return pltpu.CompilerParams(dimension_semantics=sem, vmem_limit_bytes=VMEM_LIMIT)


def _split_bf16(x):
    hi = x.astype(BF16)
    lo = (x - hi.astype(F32)).astype(BF16)
    return hi, lo


def _norm_mod(x, g, shift, scale):
    ms = jnp.mean(x * x, axis=-1, keepdims=True)
    y = x * lax.rsqrt(ms + EPS) * g
    return y * (1.0 + scale) + shift


def _adaln_kernel(c_ref, w_ref, b_ref, o_ref):
    c = c_ref[...]
    a = (c * jax.nn.sigmoid(c)).astype(BF16)
    o_ref[...] = jnp.dot(a, w_ref[...].astype(BF16), preferred_element_type=F32) + b_ref[...]


def _adaln(c, w, b, tn=1024):
    m, d = c.shape
    n = w.shape[1]
    return pl.pallas_call(
        _adaln_kernel,
        grid=(n // tn,),
        in_specs=[pl.BlockSpec((m, d), lambda j: (0, 0)),
                  pl.BlockSpec((d, tn), lambda j: (0, j)),
                  pl.BlockSpec((1, tn), lambda j: (0, j))],
        out_specs=pl.BlockSpec((m, tn), lambda j: (0, j)),
        out_shape=jax.ShapeDtypeStruct((m, n), F32),
        compiler_params=_params("arbitrary"),
        name="adaln",
    )(c, w, b.reshape(1, n))


def _rope_tables(pos):
    rot = 16
    lane = np.arange(LANES)
    sub = lane % 64
    inv = ROPE_THETA ** (-jnp.arange(0, rot, 2, dtype=F32) / rot)
    ang = pos.astype(F32)[:, None] * inv
    ang = ang[:, sub % 8]
    cos, sin = jnp.cos(ang), jnp.sin(ang)
    first = jnp.asarray(sub < 8)
    second = jnp.asarray((sub >= 8) & (sub < 16))
    c = jnp.where(first | second, cos, 1.0)
    s_lo = jnp.where(first, -sin, 0.0)
    s_hi = jnp.where(second, sin, 0.0)
    return c, s_lo, s_hi


def _prenorm_kernel(x_ref, g_ref, sh_ref, sc_ref, h_ref):
    h_ref[...] = _norm_mod(x_ref[...], g_ref[...], sh_ref[...], sc_ref[...]).astype(BF16)


def _prenorm(x, g, mods, tb, blocks_per_batch):
    t, d = x.shape
    nb, r, _ = mods.shape
    bmap = lambda i: i // blocks_per_batch
    mod_spec = lambda k: pl.BlockSpec((None, r, d), lambda i: (bmap(i), 0, k))
    tok = pl.BlockSpec((tb, d), lambda i: (i, 0))
    return pl.pallas_call(
        _prenorm_kernel,
        grid=(t // tb,),
        in_specs=[tok, pl.BlockSpec((1, d), lambda i: (0, 0)), mod_spec(0), mod_spec(1)],
        out_specs=tok,
        out_shape=jax.ShapeDtypeStruct((t, d), BF16),
        compiler_params=_params("arbitrary"),
        name="prenorm",
    )(x, g.reshape(1, d), mods, mods)


def _inproj_kernel(h_ref, w_ref, cos_ref, slo_ref, shi_ref, o_ref, w_scr, *, rope_blocks):
    n = pl.program_id(0)

    @pl.when(pl.program_id(1) == 0)
    def _():
        w_scr[...] = w_ref[...].astype(BF16)

    rope = functools.reduce(jnp.logical_or, [n == b for b in rope_blocks])

    @pl.when(jnp.logical_not(rope))
    def _():
        o_ref[...] = jnp.dot(h_ref[...], w_scr[...], preferred_element_type=F32)

    @pl.when(rope)
    def _():
        cw = 2 * LANES
        cos, slo, shi = (jnp.tile(r[...], (1, cw // LANES)) for r in (cos_ref, slo_ref, shi_ref))
        for j in range(w_scr.shape[1] // cw):
            zc = jnp.dot(h_ref[...], w_scr[:, j * cw:(j + 1) * cw], preferred_element_type=F32)
            o_ref[:, j * cw:(j + 1) * cw] = (
                zc * cos + pltpu.roll(zc, cw - 8, 1) * slo + pltpu.roll(zc, 8, 1) * shi)


def _inproj(h, w, pos, tb):
    t, d = h.shape
    wn = 1024
    nblk = w.shape[1] // wn
    assert nblk == 6
    cos, slo, shi = _rope_tables(pos)
    npos = pos.shape[0] // tb
    tab_spec = pl.BlockSpec((tb, LANES), lambda n, i: (i % npos, 0))
    return pl.pallas_call(
        functools.partial(_inproj_kernel, rope_blocks=(3, 4)),
        grid=(nblk, t // tb),
        in_specs=[pl.BlockSpec((tb, d), lambda n, i: (i, 0)),
                  pl.BlockSpec((d, wn), lambda n, i: (0, n)),
                  tab_spec, tab_spec, tab_spec],
        out_specs=pl.BlockSpec((None, tb, wn), lambda n, i: (n, i, 0)),
        out_shape=jax.ShapeDtypeStruct((nblk, t, wn), F32),
        scratch_shapes=[pltpu.VMEM((d, wn), BF16)],
        compiler_params=_params("arbitrary", "arbitrary"),
        name="inproj",
    )(h, w, cos, slo, shi)


def _head_rows_kernel(x_ref, o_ref, *, heads, hd):
    tb = x_ref.shape[0]
    for h in range(heads):
        o_ref[pl.ds(h, tb, stride=heads), :] = x_ref[:, h * hd:(h + 1) * hd]


def _head_rows(proj, slab, heads, tb):
    _, t, w = proj.shape
    hd = w // heads
    return pl.pallas_call(
        functools.partial(_head_rows_kernel, heads=heads, hd=hd),
        grid=(t // tb,),
        in_specs=[pl.BlockSpec((None, tb, w), lambda i: (slab, i, 0))],
        out_specs=pl.BlockSpec((tb * heads, hd), lambda i: (i, 0)),
        out_shape=jax.ShapeDtypeStruct((t * heads, hd), F32),
        compiler_params=_params("arbitrary"),
        name="kv_head_rows",
    )(proj)


def _token_minor_kernel(x_ref, o_ref):
    o_ref[...] = x_ref[...].T


def _token_minor(proj, slab, batch, tb):
    _, t, w = proj.shape
    nblk = t // batch // tb
    return pl.pallas_call(
        _token_minor_kernel,
        grid=(batch, nblk),
        in_specs=[pl.BlockSpec((None, tb, w), lambda b, i: (slab, b * nblk + i, 0))],
        out_specs=pl.BlockSpec((None, w, tb), lambda b, i: (b, 0, i)),
        out_shape=jax.ShapeDtypeStruct((batch, w, t // batch), F32),
        compiler_params=_params("arbitrary", "arbitrary"),
        name="k_token_minor",
    )(proj)


def _softplus(z):
    return jnp.maximum(z, 0.0) + jnp.log(1.0 + jnp.exp(-jnp.abs(z)))


def _suffix_matrix(n):
    s = lax.broadcasted_iota(jnp.int32, (n, n), 0)
    j = lax.broadcasted_iota(jnp.int32, (n, n), 1)
    return jnp.where(s > j, 1.0, 0.0).astype(BF16)


def _lane_tile(x, width):
    return jnp.tile(x, (1, width // LANES))


def _sb_chunk(z, v_bf, c, mask):
    sp = _softplus(z)
    lk = -sp if mask is None else jnp.where(mask, -sp, 0.0)
    lk_hi, lk_lo = _split_bf16(lk)
    u = _suffix_matrix(z.shape[1])
    after = (jnp.dot(lk_hi, u, preferred_element_type=F32)
             + jnp.dot(lk_lo, u, preferred_element_type=F32))
    a = jnp.exp((z - sp) + (after + _lane_tile(c, z.shape[1])))
    if mask is not None:
        a = jnp.where(mask, a, 0.0)
    contrib = jnp.dot(a.astype(BF16), v_bf, preferred_element_type=F32)
    return contrib, c + (after[:, 0:1] + lk[:, 0:1])


def _sb_prompt_kernel(qb_ref, kb_ref, q_ref, k_ref, v_ref, o_ref, acc_ref, c_ref, *, tq, sub, scale):
    p = pl.program_id(2)
    qb, kb = qb_ref[p], kb_ref[p]

    def step(diagonal):
        q = q_ref[...].astype(BF16)
        if diagonal:
            acc, c = jnp.zeros(acc_ref.shape, F32), jnp.zeros(c_ref.shape, F32)
            row = lax.broadcasted_iota(jnp.int32, (tq, sub), 0)
            col = lax.broadcasted_iota(jnp.int32, (tq, sub), 1)
        else:
            acc, c = acc_ref[...], c_ref[...]
        for s in reversed(range(tq // sub)):
            k = k_ref[s * sub:(s + 1) * sub, :].astype(BF16)
            v = v_ref[s * sub:(s + 1) * sub, :].astype(BF16)
            z = lax.dot_general(q, k, _NT, preferred_element_type=F32) * scale
            mask = (col + s * sub) < row if diagonal else None
            contrib, c = _sb_chunk(z, v, c, mask)
            acc = acc + contrib
        acc_ref[...] = acc
        c_ref[...] = c

    pl.when(kb == qb)(lambda: step(True))
    pl.when(kb != qb)(lambda: step(False))

    @pl.when(kb == 0)
    def _():
        o_ref[...] = acc_ref[...]


def _tri_pairs(nq):
    qs, ks = [], []
    for q in range(nq):
        for k in range(q, -1, -1):
            qs.append(q)
            ks.append(k)
    return jnp.asarray(qs, jnp.int32), jnp.asarray(ks, jnp.int32)


def _qkv_specs(first, nq, tq, hd):
    qspec = pl.BlockSpec((None, tq, hd), lambda b, h, p, qa, ka: (first, b * nq + qa[p], h))
    kspec = pl.BlockSpec((None, tq, hd), lambda b, h, p, qa, ka: (first + 1, b * nq + ka[p], h))
    vspec = pl.BlockSpec((None, tq, hd), lambda b, h, p, qa, ka: (first + 2, b * nq + ka[p], h))
    return [qspec, kspec, vspec]


def _sb_prompt(proj, batch, heads, tq, sub):
    _, t, w = proj.shape
    hd = w // heads
    nq = t // batch // tq
    qs, ks = _tri_pairs(nq)
    return pl.pallas_call(
        functools.partial(_sb_prompt_kernel, tq=tq, sub=sub, scale=1.0 / math.sqrt(hd)),
        grid_spec=pltpu.PrefetchScalarGridSpec(
            num_scalar_prefetch=2,
            grid=(batch, heads, qs.shape[0]),
            in_specs=_qkv_specs(0, nq, tq, hd),
            out_specs=pl.BlockSpec((tq, hd), lambda b, h, p, qa, ka: (b * nq + qa[p], h)),
            scratch_shapes=[pltpu.VMEM((tq, hd), F32), pltpu.VMEM((tq, LANES), F32)]),
        out_shape=jax.ShapeDtypeStruct((t, w), F32),
        compiler_params=_params("arbitrary", "arbitrary", "arbitrary"),
        name="sb_prompt",
    )(qs, ks, proj, proj, proj)


def _lambda(lp, lam_init):
    a = jnp.sum(lp[0:1, :] * lp[1:2, :], axis=1, keepdims=True)
    b = jnp.sum(lp[2:3, :] * lp[3:4, :], axis=1, keepdims=True)
    return jnp.exp(a) - jnp.exp(b) + lam_init


def _diff_prompt_kernel(qb_ref, kb_ref, q_ref, k_ref, v_ref, lam_ref, o_ref,
                        q2_scr, acc_ref, m_ref, l_ref, *, tq, dqk, scale, lam_init):
    p = pl.program_id(2)
    qb, kb = qb_ref[p], kb_ref[p]
    fold = math.frexp(scale)[0] == 0.5

    def step(diagonal):
        if diagonal:
            q = q_ref[...] * scale if fold else q_ref[...]
            lane = lax.broadcasted_iota(jnp.int32, q.shape, 1)
            q2 = jnp.concatenate([jnp.where(lane < dqk, q, 0.0), jnp.where(lane >= dqk, q, 0.0)],
                                 axis=0).astype(BF16)
            q2_scr[...] = q2
        else:
            q2 = q2_scr[...]
        s = lax.dot_general(q2, k_ref[...].astype(BF16), _NT, preferred_element_type=F32)
        if not fold:
            s = s * scale
        v = v_ref[...].astype(BF16)
        if diagonal:
            row = lax.broadcasted_iota(jnp.int32, s.shape, 0)
            col = lax.broadcasted_iota(jnp.int32, s.shape, 1)
            s = jnp.where(col <= jnp.where(row >= tq, row - tq, row), s, -jnp.inf)
            m_new = jnp.broadcast_to(jnp.max(s, axis=1, keepdims=True), m_ref.shape)
            pr = jnp.exp(s - _lane_tile(m_new, tq))
            l_ref[...] = jnp.broadcast_to(jnp.sum(pr, axis=1, keepdims=True), l_ref.shape)
            acc_ref[...] = jnp.dot(pr.astype(BF16), v, preferred_element_type=F32)
        else:
            m_old = m_ref[...]
            m_new = jnp.maximum(m_old, jnp.max(s, axis=1, keepdims=True))
            alpha = jnp.exp(m_old - m_new)
            pr = jnp.exp(s - _lane_tile(m_new, tq))
            l_ref[...] = alpha * l_ref[...] + jnp.sum(pr, axis=1, keepdims=True)
            acc_ref[...] = alpha * acc_ref[...] + jnp.dot(pr.astype(BF16), v, preferred_element_type=F32)
        m_ref[...] = m_new

    pl.when(kb == qb)(lambda: step(True))
    pl.when(kb != qb)(lambda: step(False))

    @pl.when(kb == 0)
    def _():
        lam = _lambda(lam_ref[...], lam_init)
        nrm = acc_ref[...] / l_ref[...]
        o_ref[...] = nrm[:tq] - lam * nrm[tq:]


def _diff_prompt(proj, lam_p, lam_init, batch, heads, tq):
    _, t, w = proj.shape
    hd = w // heads
    dqk = hd // 2
    nq = t // batch // tq
    qs, ks = _tri_pairs(nq)
    return pl.pallas_call(
        functools.partial(_diff_prompt_kernel, tq=tq, dqk=dqk, scale=1.0 / math.sqrt(dqk),
                          lam_init=lam_init),
        grid_spec=pltpu.PrefetchScalarGridSpec(
            num_scalar_prefetch=2,
            grid=(batch, heads, qs.shape[0]),
            in_specs=_qkv_specs(3, nq, tq, hd)
                     + [pl.BlockSpec(lam_p.shape, lambda b, h, p, qa, ka: (0, 0))],
            out_specs=pl.BlockSpec((tq, hd), lambda b, h, p, qa, ka: (b * nq + qa[p], h)),
            scratch_shapes=[pltpu.VMEM((2 * tq, hd), BF16), pltpu.VMEM((2 * tq, hd), F32),
                            pltpu.VMEM((2 * tq, LANES), F32), pltpu.VMEM((2 * tq, LANES), F32)]),
        out_shape=jax.ShapeDtypeStruct((t, w), F32),
        compiler_params=_params("arbitrary", "arbitrary", "arbitrary"),
        name="diff_prompt",
    )(qs, ks, proj, proj, proj, lam_p)


def _sample_attn_kernel(pt_ref, sbq_ref, dq_ref, dkn_ref, dvn_ref, *rest, heads, hd, lam_init, pps):
    ksb_refs, vsb_refs, kdt_refs, vd_refs = (rest[i * pps:(i + 1) * pps] for i in range(4))
    (lam_ref, osb_ref, od_ref, qsb_scr, qd_scr, acc_sb, c_scr, acc_d, m_scr, l_scr) = rest[4 * pps:]
    j = pl.program_id(1)
    w = heads * hd
    dqk = hd // 2
    rows = 2 * heads
    page = kdt_refs[0].shape[1]
    rr = lax.broadcasted_iota(jnp.int32, (rows, hd), 0)

    @pl.when(j == 0)
    def _():
        r_i = lax.broadcasted_iota(jnp.int32, (rows, w), 0)
        l_i = lax.broadcasted_iota(jnp.int32, (rows, w), 1)
        d_mask = (r_i % heads == l_i // hd) & (r_i // heads == (l_i // dqk) % 2)
        qd = jnp.where(d_mask, dq_ref[...], 0.0).astype(BF16)
        qd_scr[...] = qd
        qsb_scr[...] = jnp.concatenate([sbq_ref[...], jnp.zeros((heads, hd), F32)], axis=0).astype(BF16)
        acc_sb[...] = jnp.zeros_like(acc_sb)
        c_scr[...] = jnp.zeros_like(c_scr)
        kn = dkn_ref[...].astype(BF16).astype(F32)
        vn = dvn_ref[...].astype(BF16).astype(F32)
        m_scr[...] = jnp.sum(qd.astype(F32) * kn, axis=1, keepdims=True) * (1.0 / math.sqrt(dqk))
        l_scr[...] = jnp.ones_like(l_scr)
        acc_d[...] = jnp.concatenate([vn, vn], axis=0)

    def head_rows(ref, h):
        return ref[pl.ds(h, page, stride=heads), :].astype(BF16)

    def head_rows_all(refs, h):
        return jnp.concatenate([head_rows(r, h) for r in refs], axis=0)

    qsb = qsb_scr[...]
    rw = lax.broadcasted_iota(jnp.int32, (rows, pps * page), 0)

    z = jnp.zeros((rows, pps * page), F32)
    for h in range(heads):
        zh = lax.dot_general(qsb, head_rows_all(ksb_refs, h), _NT, preferred_element_type=F32)
        z = jnp.where(rw == h, zh, z)
    z = z * (1.0 / math.sqrt(hd))
    sp = _softplus(z)
    ls = z - sp
    chunks = [-sp[:, p * page:(p + 1) * page] for p in range(pps)]
    parts = [x for lk in chunks for x in _split_bf16(lk)]
    within = jnp.dot(jnp.concatenate(parts, axis=0), _suffix_matrix(page), preferred_element_type=F32)
    c = c_scr[...]
    a_parts = []
    for p in range(pps):
        after = within[2 * p * rows:(2 * p + 1) * rows] + within[(2 * p + 1) * rows:(2 * p + 2) * rows]
        a_parts.append(jnp.exp(ls[:, p * page:(p + 1) * page] + (after + c)))
        c = c + after[:, 0:1] + chunks[p][:, 0:1]
    c_scr[...] = c
    a = jnp.concatenate(a_parts, axis=1).astype(BF16)
    asb = acc_sb[...]
    for h in range(heads):
        oh = jnp.dot(a, head_rows_all(vsb_refs, h), preferred_element_type=F32)
        asb = asb + jnp.where(rr == h, oh, 0.0)
    acc_sb[...] = asb

    kdt = jnp.concatenate([r[...].astype(BF16) for r in kdt_refs], axis=1)
    s = jnp.dot(qd_scr[...], kdt, preferred_element_type=F32) * (1.0 / math.sqrt(dqk))
    m = m_scr[...]
    m_new = jnp.maximum(m, jnp.max(s, axis=1, keepdims=True))
    alpha = jnp.exp(m - m_new)
    pr = jnp.exp(s - m_new)
    l = alpha * l_scr[...] + jnp.sum(pr, axis=1, keepdims=True)
    pr = pr.astype(BF16)
    ad = alpha * acc_d[...]
    for h in range(heads):
        oh = jnp.dot(pr, head_rows_all(vd_refs, h), preferred_element_type=F32)
        ad = ad + jnp.where(rr % heads == h, oh, 0.0)
    m_scr[...], l_scr[...], acc_d[...] = m_new, l, ad

    @pl.when(j == pl.num_programs(1) - 1)
    def _():
        osb_ref[...] = asb[:heads, :]
        lam = _lambda(lam_ref[...], lam_init)
        nrm = ad / l
        od_ref[...] = nrm[:heads, :] - lam * nrm[heads:, :]


def _sample_attn(sbq, dq, dkn, dvn, c_sbk, c_sbv, c_dkt, c_dv, page_table, lam_p, lam_init, heads, pps):
    n, w = sbq.shape
    hd = w // heads
    npages = page_table.shape[1]
    page = c_dkt.shape[2]
    rows = 2 * heads
    assert npages % pps == 0
    row_spec = pl.BlockSpec((None, 1, w), lambda b, j, pt: (b, 0, 0))
    head_spec = pl.BlockSpec((None, heads, hd), lambda b, j, pt: (b, 0, 0))

    def cache_specs(shape):
        return [pl.BlockSpec((None,) + shape, lambda b, j, pt, p=p: (pt[b, npages - 1 - (j * pps + p)], 0, 0))
                for p in range(pps)]

    kv_specs = cache_specs((page * heads, hd))
    osb, od = pl.pallas_call(
        functools.partial(_sample_attn_kernel, heads=heads, hd=hd, lam_init=lam_init, pps=pps),
        grid_spec=pltpu.PrefetchScalarGridSpec(
            num_scalar_prefetch=1,
            grid=(n, npages // pps),
            in_specs=[head_spec, row_spec, row_spec, head_spec]
                     + kv_specs + kv_specs + cache_specs((w, page)) + kv_specs
                     + [pl.BlockSpec(lam_p.shape, lambda b, j, pt: (0, 0))],
            out_specs=[head_spec, head_spec],
            scratch_shapes=[pltpu.VMEM((rows, hd), BF16), pltpu.VMEM((rows, w), BF16),
                            pltpu.VMEM((rows, hd), F32), pltpu.VMEM((rows, 1), F32),
                            pltpu.VMEM((rows, hd), F32), pltpu.VMEM((rows, 1), F32),
                            pltpu.VMEM((rows, 1), F32)]),
        out_shape=[jax.ShapeDtypeStruct((n, heads, hd), F32)] * 2,
        compiler_params=_params("arbitrary", "arbitrary"),
        name="sample_attn",
    )(page_table, sbq.reshape(n, heads, hd), dq.reshape(n, 1, w), dkn.reshape(n, 1, w),
      dvn.reshape(n, heads, hd), *([c_sbk] * pps), *([c_sbv] * pps), *([c_dkt] * pps), *([c_dv] * pps),
      lam_p)
    return osb.reshape(n, w), od.reshape(n, w)


def _merge_kernel(sbo_ref, do_ref, x_ref, w_ref, dg_ref, gate_ref, g2_ref, sh_ref, sc_ref,
                  x2_ref, hhi_ref, hlo_ref, o_scr, *, heads, hd, lam_init):
    sbw = heads * hd
    o_scr[:, :sbw] = sbo_ref[...].astype(BF16)
    dg = dg_ref[...]
    for h in range(heads):
        d = do_ref[:, h * hd:(h + 1) * hd]
        ms = jnp.mean(d * d, axis=-1, keepdims=True)
        dn = (d * lax.rsqrt(ms + EPS) * dg) * (1.0 - lam_init)
        o_scr[:, sbw + h * hd:sbw + (h + 1) * hd] = dn.astype(BF16)
    o = jnp.dot(o_scr[...], w_ref[...], preferred_element_type=F32)
    x2 = x_ref[...] + gate_ref[...] * o
    x2_ref[...] = x2
    h2 = _norm_mod(x2, g2_ref[...], sh_ref[...], sc_ref[...])
    hhi_ref[...], hlo_ref[...] = _split_bf16(h2)


def _merge(sbo, do, x, w_bf, dg, g2, mods, lam_init, heads, tb, blocks_per_batch):
    t, d = x.shape
    nb, r, _ = mods.shape
    w = sbo.shape[1]
    hd = w // heads
    bmap = lambda i: i // blocks_per_batch
    mod_spec = lambda k: pl.BlockSpec((None, r, d), lambda i: (bmap(i), 0, k))
    tok = lambda width: pl.BlockSpec((tb, width), lambda i: (i, 0))
    return pl.pallas_call(
        functools.partial(_merge_kernel, heads=heads, hd=hd, lam_init=lam_init),
        grid=(t // tb,),
        in_specs=[tok(w), tok(w), tok(d),
                  pl.BlockSpec(w_bf.shape, lambda i: (0, 0)),
                  pl.BlockSpec((1, hd), lambda i: (0, 0)),
                  mod_spec(2),
                  pl.BlockSpec((1, d), lambda i: (0, 0)),
                  mod_spec(3), mod_spec(4)],
        out_specs=[tok(d), tok(d), tok(d)],
        out_shape=[jax.ShapeDtypeStruct((t, d), F32)] + [jax.ShapeDtypeStruct((t, d), BF16)] * 2,
        scratch_shapes=[pltpu.VMEM((tb, 2 * w), BF16)],
        compiler_params=_params("arbitrary"),
        name="merge_outproj",
    )(sbo, do, x, w_bf, dg.reshape(1, hd), mods, g2.reshape(1, d), mods, mods)


def _topk_rows(s, k):
    n = s.shape[0]
    idx = lax.broadcasted_iota(jnp.int32, s.shape, 0).astype(F32)
    vals, ids = [], []
    for _ in range(k):
        m = jnp.max(s, axis=0, keepdims=True)
        i = jnp.min(jnp.where(s == m, idx, float(n)), axis=0, keepdims=True)
        s = jnp.where(idx == i, -jnp.inf, s)
        vals.append(m)
        ids.append(i)
    return jnp.concatenate(vals, axis=0), jnp.concatenate(ids, axis=0)


def _route_head(q, khi_ref, klo_ref, half, topk):
    tops = []
    for c in range(2):
        q_hi, q_lo = _split_bf16(q[:, c * half:(c + 1) * half])
        k_hi, k_lo = khi_ref[c], klo_ref[c]
        nt = functools.partial(lax.dot_general, dimension_numbers=_NT, preferred_element_type=F32)
        s = nt(k_hi, q_hi) + nt(k_hi, q_lo) + nt(k_lo, q_hi)
        tops.append(_topk_rows(s, topk))
    (v0, i0), (v1, i1) = tops
    tb = v0.shape[1]
    sub8 = lax.broadcasted_iota(jnp.int32, (8, tb), 0).astype(F32)
    blocks, positions = [], []
    for a0, b in [(0, 0), (8, 0)] + [(0, b) for b in range(1, 8)]:
        blocks.append(v0[a0:a0 + 8, :] + v1[b:b + 1, :])
        positions.append((sub8 + float(a0)) * float(topk) + float(b))
    blocks.append(v0[0:1, :] + v1[8:16, :])
    positions.append(sub8 + 8.0)
    cand = jnp.stack(blocks, axis=0)
    pos = jnp.stack(positions, axis=0)
    a_iota = lax.broadcasted_iota(jnp.int32, (topk, tb), 0).astype(F32)
    best, sel_i, sel_j = [], [], []
    for _ in range(topk):
        m = jnp.max(jnp.max(cand, axis=0), axis=0, keepdims=True)
        p = jnp.min(jnp.min(jnp.where(cand == m[None], pos, float(topk * topk)), axis=0),
                    axis=0, keepdims=True)
        cand = jnp.where(pos == p[None], -jnp.inf, cand)
        pa = jnp.floor(p * (1.0 / topk))
        pb = p - pa * topk
        best.append(m)
        sel_i.append(jnp.sum(jnp.where(a_iota == pa, i0, 0.0), axis=0, keepdims=True))
        sel_j.append(jnp.sum(jnp.where(a_iota == pb, i1, 0.0), axis=0, keepdims=True))
    best = jnp.concatenate(best, axis=0)
    e = jnp.exp(best - best[0:1, :])
    return (jnp.concatenate(sel_i, axis=0).astype(jnp.int32), jnp.concatenate(sel_j, axis=0).astype(jnp.int32),
            e / jnp.sum(e, axis=0, keepdims=True))


def _route_kernel(hhi_ref, hlo_ref, whi_ref, wlo_ref, khi_ref, klo_ref, i_ref, j_ref, g_ref, *, half, topk):
    h_hi, h_lo = hhi_ref[...], hlo_ref[...]
    dot = functools.partial(jnp.dot, preferred_element_type=F32)
    q = dot(h_hi, whi_ref[...]) + dot(h_hi, wlo_ref[...]) + dot(h_lo, whi_ref[...])
    for hh in range(ROUTE_HEADS):
        i_sel, j_sel, g = _route_head(q[:, hh * 2 * half:(hh + 1) * 2 * half], khi_ref.at[hh], klo_ref.at[hh],
                                      half, topk)
        i_ref[hh * topk:(hh + 1) * topk, :] = i_sel
        j_ref[hh * topk:(hh + 1) * topk, :] = j_sel
        g_ref[hh * topk:(hh + 1) * topk, :] = g


def _route(h_hi, h_lo, wq_hi, wq_lo, keys_hi, keys_lo, tb):
    t, d = h_hi.shape
    heads, _, n_keys, half = keys_hi.shape
    topk = PEER_TOPK
    assert topk == 16
    hg = ROUTE_HEADS
    assert heads % hg == 0
    out_spec = pl.BlockSpec((hg * topk, tb), lambda i, hh: (hh, i))
    key_spec = pl.BlockSpec((hg, 2, n_keys, half), lambda i, hh: (hh, 0, 0, 0))
    w_spec = pl.BlockSpec((d, hg * 2 * half), lambda i, hh: (0, hh))
    return pl.pallas_call(
        functools.partial(_route_kernel, half=half, topk=topk),
        grid=(t // tb, heads // hg),
        in_specs=[pl.BlockSpec((tb, d), lambda i, hh: (i, 0))] * 2 + [w_spec, w_spec, key_spec, key_spec],
        out_specs=[out_spec] * 3,
        out_shape=[jax.ShapeDtypeStruct((heads * topk, t), jnp.int32)] * 2
                  + [jax.ShapeDtypeStruct((heads * topk, t), F32)],
        compiler_params=_params("arbitrary", "arbitrary"),
        name="peer_route",
    )(h_hi, h_lo, wq_hi, wq_lo, keys_hi, keys_lo)


def _gelu(x):
    return 0.5 * x * (1.0 + lax.erf(x * (1.0 / math.sqrt(2.0))))


def _peer_kernel(h_ref, i_ref, j_ref, gt_ref, u_ref, v_ref, x_ref, gate_ref, gf_ref, sh_ref, sc_ref,
                 y_ref, gmat, wd_scr, *, tb, n_keys, eb):
    e = pl.program_id(1)
    per = PEER_CHUNK // n_keys

    @pl.when(e == 0)
    def _():
        y_ref[...] = jnp.zeros_like(y_ref)
        sub = lax.broadcasted_iota(jnp.int32, (n_keys, n_keys), 0)

        def build_one(t):
            irow = i_ref[pl.ds(t, 1), :]
            jrow = j_ref[pl.ds(t, 1), :]
            g = gt_ref[pl.ds(t, 1), :]
            a = jnp.where(sub == irow, g, 0.0).astype(BF16)
            b = jnp.where(sub == jrow, 1.0, 0.0).astype(BF16)
            return lax.dot_general(a, b, _NT, preferred_element_type=F32)

        def build(o, carry):
            un = PEER_BUILD_UNROLL
            g = jnp.stack([build_one(o * un + k) for k in range(un)], axis=0)
            gmat[:, pl.ds(pl.multiple_of(o * un, un), un), :] = jnp.swapaxes(g, 0, 1).astype(BF16)
            return carry

        lax.fori_loop(0, tb // PEER_BUILD_UNROLL, build, 0)

    rows = min(tb, PEER_ROWS)
    for cix in range(eb // PEER_CHUNK):
        lo = cix * PEER_CHUNK
        first = e * (eb // n_keys) + cix * per
        for r0 in range(0, tb, rows):
            s = lax.dot_general(h_ref[r0:r0 + rows, :], u_ref[lo:lo + PEER_CHUNK, :], _NT,
                                preferred_element_type=F32)
            gate = jnp.concatenate([gmat[first + r, r0:r0 + rows, :] for r in range(per)],
                                   axis=1).astype(F32)
            wd_scr[r0:r0 + rows, lo:lo + PEER_CHUNK] = (gate * _gelu(s)).astype(BF16)
    for r0 in range(0, tb, rows):
        y_ref[r0:r0 + rows, :] += jnp.dot(wd_scr[r0:r0 + rows, :], v_ref[...], preferred_element_type=F32)

    @pl.when(e == pl.num_programs(1) - 1)
    def _():
        x3 = x_ref[...] + gate_ref[...] * y_ref[...]
        y_ref[...] = _norm_mod(x3, gf_ref[...], sh_ref[...], sc_ref[...])


def _peer(h, i_sel, j_sel, g_sel, u_bf, v_bf, x, mods, fmods, gf, tb, blocks_per_batch):
    t, d = h.shape
    n_exp = u_bf.shape[0]
    n_keys = PEER_N_KEYS
    eb = PEER_EXPERT_BLOCK
    npick = i_sel.shape[1]
    nb, r, _ = mods.shape
    assert tb % PEER_BUILD_UNROLL == 0 and eb % PEER_CHUNK == 0 and PEER_CHUNK % n_keys == 0
    bmap = lambda i: i // blocks_per_batch
    tok = lambda width: pl.BlockSpec((tb, width), lambda i, e: (i, 0))
    tab = pl.BlockSpec((eb, d), lambda i, e: (e, 0))
    return pl.pallas_call(
        functools.partial(_peer_kernel, tb=tb, n_keys=n_keys, eb=eb),
        grid=(t // tb, n_exp // eb),
        in_specs=[tok(d), tok(npick), tok(npick), tok(npick), tab, tab, tok(d),
                  pl.BlockSpec((None, r, d), lambda i, e: (bmap(i), 0, 5)),
                  pl.BlockSpec((1, d), lambda i, e: (0, 0)),
                  pl.BlockSpec((None, r, d), lambda i, e: (bmap(i), 0, 0)),
                  pl.BlockSpec((None, r, d), lambda i, e: (bmap(i), 0, 1))],
        out_specs=tok(d),
        out_shape=jax.ShapeDtypeStruct((t, d), F32),
        scratch_shapes=[pltpu.VMEM((n_keys, tb, n_keys), BF16), pltpu.VMEM((tb, eb), BF16)],
        compiler_params=_params("arbitrary", "arbitrary"),
        name="peer_mix",
    )(h, i_sel, j_sel, g_sel, u_bf, v_bf, x, mods, gf.reshape(1, d), fmods, fmods)


def _token_path(x, mods, fmods, pos, tb, blocks_per_batch, attend, w):
    t = x.shape[0]
    h = _prenorm(x, w["norm_mix_g"], mods, tb, blocks_per_batch)
    proj = _inproj(h, w["w_in"], pos, min(INPROJ_TOKEN_BLOCK, t))
    sbo, do = attend(proj)
    x2, h2_hi, h2_lo = _merge(sbo, do, x, w["w_out"], w["diff_norm_g"], w["norm_ffn_g"], mods,
                    w["lam_init"], w["heads"], tb, blocks_per_batch)
    i_sel, j_sel, g_sel = _route(h2_hi, h2_lo, w["wq_hi"], w["wq_lo"], w["keys_hi"], w["keys_lo"], tb)
    tb_peer = min(PEER_TOKEN_BLOCK, t)
    y = _peer(h2_hi, i_sel.T, j_sel.T, g_sel.T, w["u"], w["v"], x2, mods, fmods,
              w["norm_final_g"], tb_peer, blocks_per_batch * tb // tb_peer)
    return y, proj


def kernel(x_prompt, x_sample, c_prompt, c_sample, cache_sb_k, cache_sb_v, cache_diff_k, cache_diff_v, page_table, norm_mix_g, norm_ffn_g, w_ada, b_ada, w_in, diff_lambda, diff_norm_g, w_out, peer_w_q, peer_sub_keys, peer_u, peer_v, norm_final_g, w_ada_final, b_ada_final):
    batch, seq, d = x_prompt.shape
    nsamp, dec_seq, _ = x_sample.shape
    depth = w_ada.shape[0]
    assert depth == 1 and dec_seq == 1
    heads = cache_sb_k.shape[3]
    hd = cache_sb_k.shape[4]
    past_len = page_table.shape[1] * cache_sb_k.shape[2]
    l = 0
    lam_init = 0.8 - 0.6 * math.exp(-0.3 * l)

    nc = batch + nsamp
    c_all = jnp.concatenate([c_prompt, c_sample, jnp.zeros((-nc % 8, d), F32)], axis=0)
    mods = _adaln(c_all, w_ada[l], b_ada[l])
    fmods = _adaln(c_all, w_ada_final, b_ada_final)
    mods_p, mods_s = mods[:batch, None, :], mods[None, batch:nc, :]
    fmods_p, fmods_s = fmods[:batch, None, :], fmods[None, batch:nc, :]

    wq_hi, wq_lo = _split_bf16(peer_w_q[l])
    keys_hi, keys_lo = _split_bf16(peer_sub_keys[l])
    w = dict(norm_mix_g=norm_mix_g[l], norm_ffn_g=norm_ffn_g[l], w_in=w_in[l],
             w_out=w_out[l].astype(BF16), diff_norm_g=diff_norm_g[l], lam_init=lam_init, heads=heads,
             wq_hi=wq_hi, wq_lo=wq_lo, keys_hi=keys_hi, keys_lo=keys_lo,
             u=peer_u[l].astype(BF16), v=peer_v[l].astype(BF16), norm_final_g=norm_final_g)
    lam_p = diff_lambda[l]

    def attend_prompt(proj):
        return (_sb_prompt(proj, batch, heads, ATTN_BLOCK, SB_CHUNK),
                _diff_prompt(proj, lam_p, lam_init, batch, heads, ATTN_BLOCK))

    def attend_sample(proj):
        sb_q, _, _, d_q, d_k, d_v = proj
        pool, page = cache_sb_k.shape[1], cache_sb_k.shape[2]
        rows = lambda c: c.reshape(pool, page * heads, hd)
        dkt = jnp.transpose(cache_diff_k, (0, 1, 3, 4, 5, 2)).reshape(pool, heads * hd, page)
        return _sample_attn(sb_q, d_q, d_k, d_v, rows(cache_sb_k), rows(cache_sb_v), dkt,
                            rows(cache_diff_v), page_table, lam_p, lam_init, heads,
                            math.gcd(PAGES_PER_STEP, page_table.shape[1]))

    tb_p = TOKEN_BLOCK
    y_p, proj_p = _token_path(x_prompt.reshape(batch * seq, d), mods_p, fmods_p, jnp.arange(seq),
                              tb_p, seq // tb_p, attend_prompt, w)
    y_s, proj_s = _token_path(x_sample.reshape(nsamp, d), mods_s, fmods_s,
                              jnp.full((nsamp,), past_len, jnp.int32), nsamp, 1, attend_sample, w)

    def rows(proj, n, t):
        sb_k, sb_v, d_k, d_v = proj[1], proj[2], proj[4], proj[5]
        return (sb_k.reshape(1, n, t, heads, hd), sb_v.reshape(1, n, t, heads, hd),
                d_k.reshape(1, n, t, heads, 2, hd // 2), d_v.reshape(1, n, t, heads, hd))

    kv_tb = min(TOKEN_BLOCK, seq)
    sb_k, sb_v, d_v = (_head_rows(proj_p, s, heads, kv_tb).reshape(1, batch, seq, heads, hd)
                       for s in (1, 2, 5))
    d_k = _token_minor(proj_p, 4, batch, min(INPROJ_TOKEN_BLOCK, seq))
    d_k = jnp.transpose(d_k.reshape(batch, heads, 2, hd // 2, seq), (0, 4, 1, 2, 3))[None]

    return ((y_p.reshape(batch, seq, d), y_s.reshape(nsamp, dec_seq, d))
            + (sb_k, sb_v, d_k, d_v) + rows(proj_s, nsamp, dec_seq))
```

```python
import functools
import math

import jax
import jax.numpy as jnp
import numpy as np
from jax import lax
from jax.experimental import pallas as pl
from jax.experimental.pallas import tpu as pltpu

F32 = jnp.float32
BF16 = jnp.bfloat16

EPS = 1e-6
ROPE_THETA = 500000.0
PEER_N_KEYS = 128
PEER_TOPK = 16
PEER_HEADS = 8
LANES = 128
VMEM_LIMIT = 56 * 1024 * 1024
TOKEN_BLOCK = 256
ATTN_BLOCK = 512
SB_CHUNK = 256
LAYOUT_TOKEN_BLOCK = 1024
PAGES_PER_STEP = 8
INPROJ_TOKEN_BLOCK = 512
PEER_TOKEN_BLOCK = 512
PEER_EXPERT_BLOCK = 512
PEER_CHUNK = 256
ROUTE_HEADS = 2
PEER_ROWS = 256
PEER_BUILD_UNROLL = 16

_NT = (((1,), (1,)), ((), ()))


def _params(*sem):
    return pltpu.CompilerParams(dimension_semantics=sem, vmem_limit_bytes=VMEM_LIMIT)


def _split_bf16(x):
    hi = x.astype(BF16)
    lo = (x - hi.astype(F32)).astype(BF16)
    return hi, lo


def _norm_mod(x, g, shift, scale):
    ms = jnp.mean(x * x, axis=-1, keepdims=True)
    y = x * lax.rsqrt(ms + EPS) * g
    return y * (1.0 + scale) + shift


def _adaln_kernel(c_ref, w_ref, b_ref, o_ref):
    c = c_ref[...]
    a = (c * jax.nn.sigmoid(c)).astype(BF16)
    o_ref[...] = jnp.dot(a, w_ref[...].astype(BF16), preferred_element_type=F32) + b_ref[...]


def _adaln(c, w, b, tn=1024):
    m, d = c.shape
    n = w.shape[1]
    return pl.pallas_call(
        _adaln_kernel,
        grid=(n // tn,),
        in_specs=[pl.BlockSpec((m, d), lambda j: (0, 0)),
                  pl.BlockSpec((d, tn), lambda j: (0, j)),
                  pl.BlockSpec((1, tn), lambda j: (0, j))],
        out_specs=pl.BlockSpec((m, tn), lambda j: (0, j)),
        out_shape=jax.ShapeDtypeStruct((m, n), F32),
        compiler_params=_params("arbitrary"),
        name="adaln",
    )(c, w, b.reshape(1, n))


def _rope_tables(pos):
    rot = 16
    lane = np.arange(LANES)
    sub = lane % 64
    inv = ROPE_THETA ** (-jnp.arange(0, rot, 2, dtype=F32) / rot)
    ang = pos.astype(F32)[:, None] * inv
    ang = ang[:, sub % 8]
    cos, sin = jnp.cos(ang), jnp.sin(ang)
    first = jnp.asarray(sub < 8)
    second = jnp.asarray((sub >= 8) & (sub < 16))
    c = jnp.where(first | second, cos, 1.0)
    s_lo = jnp.where(first, -sin, 0.0)
    s_hi = jnp.where(second, sin, 0.0)
    return c, s_lo, s_hi


def _prenorm_kernel(x_ref, g_ref, sh_ref, sc_ref, h_ref):
    h_ref[...] = _norm_mod(x_ref[...], g_ref[...], sh_ref[...], sc_ref[...]).astype(BF16)


def _prenorm(x, g, mods, tb, blocks_per_batch):
    t, d = x.shape
    nb, r, _ = mods.shape
    bmap = lambda i: i // blocks_per_batch
    mod_spec = lambda k: pl.BlockSpec((None, r, d), lambda i: (bmap(i), 0, k))
    tok = pl.BlockSpec((tb, d), lambda i: (i, 0))
    return pl.pallas_call(
        _prenorm_kernel,
        grid=(t // tb,),
        in_specs=[tok, pl.BlockSpec((1, d), lambda i: (0, 0)), mod_spec(0), mod_spec(1)],
        out_specs=tok,
        out_shape=jax.ShapeDtypeStruct((t, d), BF16),
        compiler_params=_params("arbitrary"),
        name="prenorm",
    )(x, g.reshape(1, d), mods, mods)


def _inproj_kernel(h_ref, w_ref, cos_ref, slo_ref, shi_ref, o_ref, w_scr, *, rope_blocks):
    n = pl.program_id(0)

    @pl.when(pl.program_id(1) == 0)
    def _():
        w_scr[...] = w_ref[...].astype(BF16)

    rope = functools.reduce(jnp.logical_or, [n == b for b in rope_blocks])

    @pl.when(jnp.logical_not(rope))
    def _():
        o_ref[...] = jnp.dot(h_ref[...], w_scr[...], preferred_element_type=F32)

    @pl.when(rope)
    def _():
        cw = 2 * LANES
        cos, slo, shi = (jnp.tile(r[...], (1, cw // LANES)) for r in (cos_ref, slo_ref, shi_ref))
        for j in range(w_scr.shape[1] // cw):
            zc = jnp.dot(h_ref[...], w_scr[:, j * cw:(j + 1) * cw], preferred_element_type=F32)
            o_ref[:, j * cw:(j + 1) * cw] = (
                zc * cos + pltpu.roll(zc, cw - 8, 1) * slo + pltpu.roll(zc, 8, 1) * shi)


def _inproj(h, w, pos, tb):
    t, d = h.shape
    wn = 1024
    nblk = w.shape[1] // wn
    assert nblk == 6
    cos, slo, shi = _rope_tables(pos)
    npos = pos.shape[0] // tb
    tab_spec = pl.BlockSpec((tb, LANES), lambda n, i: (i % npos, 0))
    return pl.pallas_call(
        functools.partial(_inproj_kernel, rope_blocks=(3, 4)),
        grid=(nblk, t // tb),
        in_specs=[pl.BlockSpec((tb, d), lambda n, i: (i, 0)),
                  pl.BlockSpec((d, wn), lambda n, i: (0, n)),
                  tab_spec, tab_spec, tab_spec],
        out_specs=pl.BlockSpec((None, tb, wn), lambda n, i: (n, i, 0)),
        out_shape=jax.ShapeDtypeStruct((nblk, t, wn), F32),
        scratch_shapes=[pltpu.VMEM((d, wn), BF16)],
        compiler_params=_params("arbitrary", "arbitrary"),
        name="inproj",
    )(h, w, cos, slo, shi)


def _head_rows_kernel(x_ref, o_ref, *, heads, hd):
    tb = x_ref.shape[0]
    for h in range(heads):
        o_ref[pl.ds(h, tb, stride=heads), :] = x_ref[:, h * hd:(h + 1) * hd]


def _head_rows(proj, slab, heads, tb):
    _, t, w = proj.shape
    hd = w // heads
    return pl.pallas_call(
        functools.partial(_head_rows_kernel, heads=heads, hd=hd),
        grid=(t // tb,),
        in_specs=[pl.BlockSpec((None, tb, w), lambda i: (slab, i, 0))],
        out_specs=pl.BlockSpec((tb * heads, hd), lambda i: (i, 0)),
        out_shape=jax.ShapeDtypeStruct((t * heads, hd), F32),
        compiler_params=_params("arbitrary"),
        name="kv_head_rows",
    )(proj)


def _token_minor_kernel(x_ref, o_ref):
    o_ref[...] = x_ref[...].T


def _token_minor(proj, slab, batch, tb):
    _, t, w = proj.shape
    nblk = t // batch // tb
    return pl.pallas_call(
        _token_minor_kernel,
        grid=(batch, nblk),
        in_specs=[pl.BlockSpec((None, tb, w), lambda b, i: (slab, b * nblk + i, 0))],
        out_specs=pl.BlockSpec((None, w, tb), lambda b, i: (b, 0, i)),
        out_shape=jax.ShapeDtypeStruct((batch, w, t // batch), F32),
        compiler_params=_params("arbitrary", "arbitrary"),
        name="k_token_minor",
    )(proj)


def _softplus(z):
    return jnp.maximum(z, 0.0) + jnp.log(1.0 + jnp.exp(-jnp.abs(z)))


def _suffix_matrix(n):
    s = lax.broadcasted_iota(jnp.int32, (n, n), 0)
    j = lax.broadcasted_iota(jnp.int32, (n, n), 1)
    return jnp.where(s > j, 1.0, 0.0).astype(BF16)


def _lane_tile(x, width):
    return jnp.tile(x, (1, width // LANES))


def _sb_chunk(z, v_bf, c, mask):
    sp = _softplus(z)
    lk = -sp if mask is None else jnp.where(mask, -sp, 0.0)
    lk_hi, lk_lo = _split_bf16(lk)
    u = _suffix_matrix(z.shape[1])
    after = (jnp.dot(lk_hi, u, preferred_element_type=F32)
             + jnp.dot(lk_lo, u, preferred_element_type=F32))
    a = jnp.exp((z - sp) + (after + _lane_tile(c, z.shape[1])))
    if mask is not None:
        a = jnp.where(mask, a, 0.0)
    contrib = jnp.dot(a.astype(BF16), v_bf, preferred_element_type=F32)
    return contrib, c + (after[:, 0:1] + lk[:, 0:1])


def _sb_prompt_kernel(qb_ref, kb_ref, q_ref, k_ref, v_ref, o_ref, acc_ref, c_ref, *, tq, sub, scale):
    p = pl.program_id(2)
    qb, kb = qb_ref[p], kb_ref[p]

    def step(diagonal):
        q = q_ref[...].astype(BF16)
        if diagonal:
            acc, c = jnp.zeros(acc_ref.shape, F32), jnp.zeros(c_ref.shape, F32)
            row = lax.broadcasted_iota(jnp.int32, (tq, sub), 0)
            col = lax.broadcasted_iota(jnp.int32, (tq, sub), 1)
        else:
            acc, c = acc_ref[...], c_ref[...]
        for s in reversed(range(tq // sub)):
            k = k_ref[s * sub:(s + 1) * sub, :].astype(BF16)
            v = v_ref[s * sub:(s + 1) * sub, :].astype(BF16)
            z = lax.dot_general(q, k, _NT, preferred_element_type=F32) * scale
            mask = (col + s * sub) < row if diagonal else None
            contrib, c = _sb_chunk(z, v, c, mask)
            acc = acc + contrib
        acc_ref[...] = acc
        c_ref[...] = c

    pl.when(kb == qb)(lambda: step(True))
    pl.when(kb != qb)(lambda: step(False))

    @pl.when(kb == 0)
    def _():
        o_ref[...] = acc_ref[...]


def _tri_pairs(nq):
    qs, ks = [], []
    for q in range(nq):
        for k in range(q, -1, -1):
            qs.append(q)
            ks.append(k)
    return jnp.asarray(qs, jnp.int32), jnp.asarray(ks, jnp.int32)


def _qkv_specs(first, nq, tq, hd):
    qspec = pl.BlockSpec((None, tq, hd), lambda b, h, p, qa, ka: (first, b * nq + qa[p], h))
    kspec = pl.BlockSpec((None, tq, hd), lambda b, h, p, qa, ka: (first + 1, b * nq + ka[p], h))
    vspec = pl.BlockSpec((None, tq, hd), lambda b, h, p, qa, ka: (first + 2, b * nq + ka[p], h))
    return [qspec, kspec, vspec]


def _sb_prompt(proj, batch, heads, tq, sub):
    _, t, w = proj.shape
    hd = w // heads
    nq = t // batch // tq
    qs, ks = _tri_pairs(nq)
    return pl.pallas_call(
        functools.partial(_sb_prompt_kernel, tq=tq, sub=sub, scale=1.0 / math.sqrt(hd)),
        grid_spec=pltpu.PrefetchScalarGridSpec(
            num_scalar_prefetch=2,
            grid=(batch, heads, qs.shape[0]),
            in_specs=_qkv_specs(0, nq, tq, hd),
            out_specs=pl.BlockSpec((tq, hd), lambda b, h, p, qa, ka: (b * nq + qa[p], h)),
            scratch_shapes=[pltpu.VMEM((tq, hd), F32), pltpu.VMEM((tq, LANES), F32)]),
        out_shape=jax.ShapeDtypeStruct((t, w), F32),
        compiler_params=_params("arbitrary", "arbitrary", "arbitrary"),
        name="sb_prompt",
    )(qs, ks, proj, proj, proj)


def _lambda(lp, lam_init):
    a = jnp.sum(lp[0:1, :] * lp[1:2, :], axis=1, keepdims=True)
    b = jnp.sum(lp[2:3, :] * lp[3:4, :], axis=1, keepdims=True)
    return jnp.exp(a) - jnp.exp(b) + lam_init


def _diff_prompt_kernel(qb_ref, kb_ref, q_ref, k_ref, v_ref, lam_ref, o_ref,
                        q2_scr, acc_ref, m_ref, l_ref, *, tq, dqk, scale, lam_init):
    p = pl.program_id(2)
    qb, kb = qb_ref[p], kb_ref[p]
    fold = math.frexp(scale)[0] == 0.5

    def step(diagonal):
        if diagonal:
            q = q_ref[...] * scale if fold else q_ref[...]
            lane = lax.broadcasted_iota(jnp.int32, q.shape, 1)
            q2 = jnp.concatenate([jnp.where(lane < dqk, q, 0.0), jnp.where(lane >= dqk, q, 0.0)],
                                 axis=0).astype(BF16)
            q2_scr[...] = q2
        else:
            q2 = q2_scr[...]
        s = lax.dot_general(q2, k_ref[...].astype(BF16), _NT, preferred_element_type=F32)
        if not fold:
            s = s * scale
        v = v_ref[...].astype(BF16)
        if diagonal:
            row = lax.broadcasted_iota(jnp.int32, s.shape, 0)
            col = lax.broadcasted_iota(jnp.int32, s.shape, 1)
            s = jnp.where(col <= jnp.where(row >= tq, row - tq, row), s, -jnp.inf)
            m_new = jnp.broadcast_to(jnp.max(s, axis=1, keepdims=True), m_ref.shape)
            pr = jnp.exp(s - _lane_tile(m_new, tq))
            l_ref[...] = jnp.broadcast_to(jnp.sum(pr, axis=1, keepdims=True), l_ref.shape)
            acc_ref[...] = jnp.dot(pr.astype(BF16), v, preferred_element_type=F32)
        else:
            m_old = m_ref[...]
            m_new = jnp.maximum(m_old, jnp.max(s, axis=1, keepdims=True))
            alpha = jnp.exp(m_old - m_new)
            pr = jnp.exp(s - _lane_tile(m_new, tq))
            l_ref[...] = alpha * l_ref[...] + jnp.sum(pr, axis=1, keepdims=True)
            acc_ref[...] = alpha * acc_ref[...] + jnp.dot(pr.astype(BF16), v, preferred_element_type=F32)
        m_ref[...] = m_new

    pl.when(kb == qb)(lambda: step(True))
    pl.when(kb != qb)(lambda: step(False))

    @pl.when(kb == 0)
    def _():
        lam = _lambda(lam_ref[...], lam_init)
        nrm = acc_ref[...] / l_ref[...]
        o_ref[...] = nrm[:tq] - lam * nrm[tq:]


def _diff_prompt(proj, lam_p, lam_init, batch, heads, tq):
    _, t, w = proj.shape
    hd = w // heads
    dqk = hd // 2
    nq = t // batch // tq
    qs, ks = _tri_pairs(nq)
    return pl.pallas_call(
        functools.partial(_diff_prompt_kernel, tq=tq, dqk=dqk, scale=1.0 / math.sqrt(dqk),
                          lam_init=lam_init),
        grid_spec=pltpu.PrefetchScalarGridSpec(
            num_scalar_prefetch=2,
            grid=(batch, heads, qs.shape[0]),
            in_specs=_qkv_specs(3, nq, tq, hd)
                     + [pl.BlockSpec(lam_p.shape, lambda b, h, p, qa, ka: (0, 0))],
            out_specs=pl.BlockSpec((tq, hd), lambda b, h, p, qa, ka: (b * nq + qa[p], h)),
            scratch_shapes=[pltpu.VMEM((2 * tq, hd), BF16), pltpu.VMEM((2 * tq, hd), F32),
                            pltpu.VMEM((2 * tq, LANES), F32), pltpu.VMEM((2 * tq, LANES), F32)]),
        out_shape=jax.ShapeDtypeStruct((t, w), F32),
        compiler_params=_params("arbitrary", "arbitrary", "arbitrary"),
        name="diff_prompt",
    )(qs, ks, proj, proj, proj, lam_p)


def _sample_attn_kernel(pt_ref, sbq_ref, dq_ref, dkn_ref, dvn_ref, *rest, heads, hd, lam_init, pps):
    ksb_refs, vsb_refs, kdt_refs, vd_refs = (rest[i * pps:(i + 1) * pps] for i in range(4))
    (lam_ref, osb_ref, od_ref, qsb_scr, qd_scr, acc_sb, c_scr, acc_d, m_scr, l_scr) = rest[4 * pps:]
    j = pl.program_id(1)
    w = heads * hd
    dqk = hd // 2
    rows = 2 * heads
    page = kdt_refs[0].shape[1]
    rr = lax.broadcasted_iota(jnp.int32, (rows, hd), 0)

    @pl.when(j == 0)
    def _():
        r_i = lax.broadcasted_iota(jnp.int32, (rows, w), 0)
        l_i = lax.broadcasted_iota(jnp.int32, (rows, w), 1)
        d_mask = (r_i % heads == l_i // hd) & (r_i // heads == (l_i // dqk) % 2)
        qd = jnp.where(d_mask, dq_ref[...], 0.0).astype(BF16)
        qd_scr[...] = qd
        qsb_scr[...] = jnp.concatenate([sbq_ref[...], jnp.zeros((heads, hd), F32)], axis=0).astype(BF16)
        acc_sb[...] = jnp.zeros_like(acc_sb)
        c_scr[...] = jnp.zeros_like(c_scr)
        kn = dkn_ref[...].astype(BF16).astype(F32)
        vn = dvn_ref[...].astype(BF16).astype(F32)
        m_scr[...] = jnp.sum(qd.astype(F32) * kn, axis=1, keepdims=True) * (1.0 / math.sqrt(dqk))
        l_scr[...] = jnp.ones_like(l_scr)
        acc_d[...] = jnp.concatenate([vn, vn], axis=0)

    def head_rows(ref, h):
        return ref[pl.ds(h, page, stride=heads), :].astype(BF16)

    def head_rows_all(refs, h):
        return jnp.concatenate([head_rows(r, h) for r in refs], axis=0)

    qsb = qsb_scr[...]
    rw = lax.broadcasted_iota(jnp.int32, (rows, pps * page), 0)

    z = jnp.zeros((rows, pps * page), F32)
    for h in range(heads):
        zh = lax.dot_general(qsb, head_rows_all(ksb_refs, h), _NT, preferred_element_type=F32)
        z = jnp.where(rw == h, zh, z)
    z = z * (1.0 / math.sqrt(hd))
    sp = _softplus(z)
    ls = z - sp
    chunks = [-sp[:, p * page:(p + 1) * page] for p in range(pps)]
    parts = [x for lk in chunks for x in _split_bf16(lk)]
    within = jnp.dot(jnp.concatenate(parts, axis=0), _suffix_matrix(page), preferred_element_type=F32)
    c = c_scr[...]
    a_parts = []
    for p in range(pps):
        after = within[2 * p * rows:(2 * p + 1) * rows] + within[(2 * p + 1) * rows:(2 * p + 2) * rows]
        a_parts.append(jnp.exp(ls[:, p * page:(p + 1) * page] + (after + c)))
        c = c + after[:, 0:1] + chunks[p][:, 0:1]
    c_scr[...] = c
    a = jnp.concatenate(a_parts, axis=1).astype(BF16)
    asb = acc_sb[...]
    for h in range(heads):
        oh = jnp.dot(a, head_rows_all(vsb_refs, h), preferred_element_type=F32)
        asb = asb + jnp.where(rr == h, oh, 0.0)
    acc_sb[...] = asb

    kdt = jnp.concatenate([r[...].astype(BF16) for r in kdt_refs], axis=1)
    s = jnp.dot(qd_scr[...], kdt, preferred_element_type=F32) * (1.0 / math.sqrt(dqk))
    m = m_scr[...]
    m_new = jnp.maximum(m, jnp.max(s, axis=1, keepdims=True))
    alpha = jnp.exp(m - m_new)
    pr = jnp.exp(s - m_new)
    l = alpha * l_scr[...] + jnp.sum(pr, axis=1, keepdims=True)
    pr = pr.astype(BF16)
    ad = alpha * acc_d[...]
    for h in range(heads):
        oh = jnp.dot(pr, head_rows_all(vd_refs, h), preferred_element_type=F32)
        ad = ad + jnp.where(rr % heads == h, oh, 0.0)
    m_scr[...], l_scr[...], acc_d[...] = m_new, l, ad

    @pl.when(j == pl.num_programs(1) - 1)
    def _():
        osb_ref[...] = asb[:heads, :]
        lam = _lambda(lam_ref[...], lam_init)
        nrm = ad / l
        od_ref[...] = nrm[:heads, :] - lam * nrm[heads:, :]


def _sample_attn(sbq, dq, dkn, dvn, c_sbk, c_sbv, c_dkt, c_dv, page_table, lam_p, lam_init, heads, pps):
    n, w = sbq.shape
    hd = w // heads
    npages = page_table.shape[1]
    page = c_dkt.shape[2]
    rows = 2 * heads
    assert npages % pps == 0
    row_spec = pl.BlockSpec((None, 1, w), lambda b, j, pt: (b, 0, 0))
    head_spec = pl.BlockSpec((None, heads, hd), lambda b, j, pt: (b, 0, 0))

    def cache_specs(shape):
        return [pl.BlockSpec((None,) + shape, lambda b, j, pt, p=p: (pt[b, npages - 1 - (j * pps + p)], 0, 0))
                for p in range(pps)]

    kv_specs = cache_specs((page * heads, hd))
    osb, od = pl.pallas_call(
        functools.partial(_sample_attn_kernel, heads=heads, hd=hd, lam_init=lam_init, pps=pps),
        grid_spec=pltpu.PrefetchScalarGridSpec(
            num_scalar_prefetch=1,
            grid=(n, npages // pps),
            in_specs=[head_spec, row_spec, row_spec, head_spec]
                     + kv_specs + kv_specs + cache_specs((w, page)) + kv_specs
                     + [pl.BlockSpec(lam_p.shape, lambda b, j, pt: (0, 0))],
            out_specs=[head_spec, head_spec],
            scratch_shapes=[pltpu.VMEM((rows, hd), BF16), pltpu.VMEM((rows, w), BF16),
                            pltpu.VMEM((rows, hd), F32), pltpu.VMEM((rows, 1), F32),
                            pltpu.VMEM((rows, hd), F32), pltpu.VMEM((rows, 1), F32),
                            pltpu.VMEM((rows, 1), F32)]),
        out_shape=[jax.ShapeDtypeStruct((n, heads, hd), F32)] * 2,
        compiler_params=_params("arbitrary", "arbitrary"),
        name="sample_attn",
    )(page_table, sbq.reshape(n, heads, hd), dq.reshape(n, 1, w), dkn.reshape(n, 1, w),
      dvn.reshape(n, heads, hd), *([c_sbk] * pps), *([c_sbv] * pps), *([c_dkt] * pps), *([c_dv] * pps),
      lam_p)
    return osb.reshape(n, w), od.reshape(n, w)


def _merge_kernel(sbo_ref, do_ref, x_ref, w_ref, dg_ref, gate_ref, g2_ref, sh_ref, sc_ref,
                  x2_ref, hhi_ref, hlo_ref, o_scr, *, heads, hd, lam_init):
    sbw = heads * hd
    o_scr[:, :sbw] = sbo_ref[...].astype(BF16)
    dg = dg_ref[...]
    for h in range(heads):
        d = do_ref[:, h * hd:(h + 1) * hd]
        ms = jnp.mean(d * d, axis=-1, keepdims=True)
        dn = (d * lax.rsqrt(ms + EPS) * dg) * (1.0 - lam_init)
        o_scr[:, sbw + h * hd:sbw + (h + 1) * hd] = dn.astype(BF16)
    o = jnp.dot(o_scr[...], w_ref[...], preferred_element_type=F32)
    x2 = x_ref[...] + gate_ref[...] * o
    x2_ref[...] = x2
    h2 = _norm_mod(x2, g2_ref[...], sh_ref[...], sc_ref[...])
    hhi_ref[...], hlo_ref[...] = _split_bf16(h2)


def _merge(sbo, do, x, w_bf, dg, g2, mods, lam_init, heads, tb, blocks_per_batch):
    t, d = x.shape
    nb, r, _ = mods.shape
    w = sbo.shape[1]
    hd = w // heads
    bmap = lambda i: i // blocks_per_batch
    mod_spec = lambda k: pl.BlockSpec((None, r, d), lambda i: (bmap(i), 0, k))
    tok = lambda width: pl.BlockSpec((tb, width), lambda i: (i, 0))
    return pl.pallas_call(
        functools.partial(_merge_kernel, heads=heads, hd=hd, lam_init=lam_init),
        grid=(t // tb,),
        in_specs=[tok(w), tok(w), tok(d),
                  pl.BlockSpec(w_bf.shape, lambda i: (0, 0)),
                  pl.BlockSpec((1, hd), lambda i: (0, 0)),
                  mod_spec(2),
                  pl.BlockSpec((1, d), lambda i: (0, 0)),
                  mod_spec(3), mod_spec(4)],
        out_specs=[tok(d), tok(d), tok(d)],
        out_shape=[jax.ShapeDtypeStruct((t, d), F32)] + [jax.ShapeDtypeStruct((t, d), BF16)] * 2,
        scratch_shapes=[pltpu.VMEM((tb, 2 * w), BF16)],
        compiler_params=_params("arbitrary"),
        name="merge_outproj",
    )(sbo, do, x, w_bf, dg.reshape(1, hd), mods, g2.reshape(1, d), mods, mods)


def _topk_rows(s, k):
    n = s.shape[0]
    idx = lax.broadcasted_iota(jnp.int32, s.shape, 0).astype(F32)
    vals, ids = [], []
    for _ in range(k):
        m = jnp.max(s, axis=0, keepdims=True)
        i = jnp.min(jnp.where(s == m, idx, float(n)), axis=0, keepdims=True)
        s = jnp.where(idx == i, -jnp.inf, s)
        vals.append(m)
        ids.append(i)
    return jnp.concatenate(vals, axis=0), jnp.concatenate(ids, axis=0)


def _route_head(q, khi_ref, klo_ref, half, topk):
    tops = []
    for c in range(2):
        q_hi, q_lo = _split_bf16(q[:, c * half:(c + 1) * half])
        k_hi, k_lo = khi_ref[c], klo_ref[c]
        nt = functools.partial(lax.dot_general, dimension_numbers=_NT, preferred_element_type=F32)
        s = nt(k_hi, q_hi) + nt(k_hi, q_lo) + nt(k_lo, q_hi)
        tops.append(_topk_rows(s, topk))
    (v0, i0), (v1, i1) = tops
    tb = v0.shape[1]
    sub8 = lax.broadcasted_iota(jnp.int32, (8, tb), 0).astype(F32)
    blocks, positions = [], []
    for a0, b in [(0, 0), (8, 0)] + [(0, b) for b in range(1, 8)]:
        blocks.append(v0[a0:a0 + 8, :] + v1[b:b + 1, :])
        positions.append((sub8 + float(a0)) * float(topk) + float(b))
    blocks.append(v0[0:1, :] + v1[8:16, :])
    positions.append(sub8 + 8.0)
    cand = jnp.stack(blocks, axis=0)
    pos = jnp.stack(positions, axis=0)
    a_iota = lax.broadcasted_iota(jnp.int32, (topk, tb), 0).astype(F32)
    best, sel_i, sel_j = [], [], []
    for _ in range(topk):
        m = jnp.max(jnp.max(cand, axis=0), axis=0, keepdims=True)
        p = jnp.min(jnp.min(jnp.where(cand == m[None], pos, float(topk * topk)), axis=0),
                    axis=0, keepdims=True)
        cand = jnp.where(pos == p[None], -jnp.inf, cand)
        pa = jnp.floor(p * (1.0 / topk))
        pb = p - pa * topk
        best.append(m)
        sel_i.append(jnp.sum(jnp.where(a_iota == pa, i0, 0.0), axis=0, keepdims=True))
        sel_j.append(jnp.sum(jnp.where(a_iota == pb, i1, 0.0), axis=0, keepdims=True))
    best = jnp.concatenate(best, axis=0)
    e = jnp.exp(best - best[0:1, :])
    return (jnp.concatenate(sel_i, axis=0).astype(jnp.int32), jnp.concatenate(sel_j, axis=0).astype(jnp.int32),
            e / jnp.sum(e, axis=0, keepdims=True))


def _route_kernel(hhi_ref, hlo_ref, whi_ref, wlo_ref, khi_ref, klo_ref, i_ref, j_ref, g_ref, *, half, topk):
    h_hi, h_lo = hhi_ref[...], hlo_ref[...]
    dot = functools.partial(jnp.dot, preferred_element_type=F32)
    q = dot(h_hi, whi_ref[...]) + dot(h_hi, wlo_ref[...]) + dot(h_lo, whi_ref[...])
    for hh in range(ROUTE_HEADS):
        i_sel, j_sel, g = _route_head(q[:, hh * 2 * half:(hh + 1) * 2 * half], khi_ref.at[hh], klo_ref.at[hh],
                                      half, topk)
        i_ref[hh * topk:(hh + 1) * topk, :] = i_sel
        j_ref[hh * topk:(hh + 1) * topk, :] = j_sel
        g_ref[hh * topk:(hh + 1) * topk, :] = g


def _route(h_hi, h_lo, wq_hi, wq_lo, keys_hi, keys_lo, tb):
    t, d = h_hi.shape
    heads, _, n_keys, half = keys_hi.shape
    topk = PEER_TOPK
    assert topk == 16
    hg = ROUTE_HEADS
    assert heads % hg == 0
    out_spec = pl.BlockSpec((hg * topk, tb), lambda i, hh: (hh, i))
    key_spec = pl.BlockSpec((hg, 2, n_keys, half), lambda i, hh: (hh, 0, 0, 0))
    w_spec = pl.BlockSpec((d, hg * 2 * half), lambda i, hh: (0, hh))
    return pl.pallas_call(
        functools.partial(_route_kernel, half=half, topk=topk),
        grid=(t // tb, heads // hg),
        in_specs=[pl.BlockSpec((tb, d), lambda i, hh: (i, 0))] * 2 + [w_spec, w_spec, key_spec, key_spec],
        out_specs=[out_spec] * 3,
        out_shape=[jax.ShapeDtypeStruct((heads * topk, t), jnp.int32)] * 2
                  + [jax.ShapeDtypeStruct((heads * topk, t), F32)],
        compiler_params=_params("arbitrary", "arbitrary"),
        name="peer_route",
    )(h_hi, h_lo, wq_hi, wq_lo, keys_hi, keys_lo)


def _gelu(x):
    return 0.5 * x * (1.0 + lax.erf(x * (1.0 / math.sqrt(2.0))))


def _peer_kernel(h_ref, i_ref, j_ref, gt_ref, u_ref, v_ref, x_ref, gate_ref, gf_ref, sh_ref, sc_ref,
                 y_ref, *rest, tb, n_keys, eb, emit_tables):
    if emit_tables:
        ub_ref, vb_ref, gmat, wd_scr = rest
        ub_ref[...] = u_ref[...].astype(BF16)
        vb_ref[...] = v_ref[...].astype(BF16)
        u_ref, v_ref = ub_ref, vb_ref
    else:
        gmat, wd_scr = rest
    e = pl.program_id(1)
    per = PEER_CHUNK // n_keys

    @pl.when(e == 0)
    def _():
        y_ref[...] = jnp.zeros_like(y_ref)
        sub = lax.broadcasted_iota(jnp.int32, (n_keys, n_keys), 0)

        def build_one(t):
            irow = i_ref[pl.ds(t, 1), :]
            jrow = j_ref[pl.ds(t, 1), :]
            g = gt_ref[pl.ds(t, 1), :]
            a = jnp.where(sub == irow, g, 0.0).astype(BF16)
            b = jnp.where(sub == jrow, 1.0, 0.0).astype(BF16)
            return lax.dot_general(a, b, _NT, preferred_element_type=F32)

        def build(o, carry):
            un = PEER_BUILD_UNROLL
            g = jnp.stack([build_one(o * un + k) for k in range(un)], axis=0)
            gmat[:, pl.ds(pl.multiple_of(o * un, un), un), :] = jnp.swapaxes(g, 0, 1).astype(BF16)
            return carry

        lax.fori_loop(0, tb // PEER_BUILD_UNROLL, build, 0)

    rows = min(tb, PEER_ROWS)
    for cix in range(eb // PEER_CHUNK):
        lo = cix * PEER_CHUNK
        first = e * (eb // n_keys) + cix * per
        for r0 in range(0, tb, rows):
            s = lax.dot_general(h_ref[r0:r0 + rows, :], u_ref[lo:lo + PEER_CHUNK, :], _NT,
                                preferred_element_type=F32)
            gate = jnp.concatenate([gmat[first + r, r0:r0 + rows, :] for r in range(per)],
                                   axis=1).astype(F32)
            wd_scr[r0:r0 + rows, lo:lo + PEER_CHUNK] = (gate * _gelu(s)).astype(BF16)
    for r0 in range(0, tb, rows):
        y_ref[r0:r0 + rows, :] += jnp.dot(wd_scr[r0:r0 + rows, :], v_ref[...], preferred_element_type=F32)

    @pl.when(e == pl.num_programs(1) - 1)
    def _():
        x3 = x_ref[...] + gate_ref[...] * y_ref[...]
        y_ref[...] = _norm_mod(x3, gf_ref[...], sh_ref[...], sc_ref[...])


def _peer(h, i_sel, j_sel, g_sel, u_tab, v_tab, x, mods, fmods, gf, tb, blocks_per_batch, emit_tables):
    t, d = h.shape
    n_exp = u_tab.shape[0]
    assert not emit_tables or t == tb
    n_keys = PEER_N_KEYS
    eb = PEER_EXPERT_BLOCK
    npick = i_sel.shape[1]
    nb, r, _ = mods.shape
    assert tb % PEER_BUILD_UNROLL == 0 and eb % PEER_CHUNK == 0 and PEER_CHUNK % n_keys == 0
    bmap = lambda i: i // blocks_per_batch
    tok = lambda width: pl.BlockSpec((tb, width), lambda i, e: (i, 0))
    tab = pl.BlockSpec((eb, d), lambda i, e: (e, 0))
    return pl.pallas_call(
        functools.partial(_peer_kernel, tb=tb, n_keys=n_keys, eb=eb, emit_tables=emit_tables),
        grid=(t // tb, n_exp // eb),
        in_specs=[tok(d), tok(npick), tok(npick), tok(npick), tab, tab, tok(d),
                  pl.BlockSpec((None, r, d), lambda i, e: (bmap(i), 0, 5)),
                  pl.BlockSpec((1, d), lambda i, e: (0, 0)),
                  pl.BlockSpec((None, r, d), lambda i, e: (bmap(i), 0, 0)),
                  pl.BlockSpec((None, r, d), lambda i, e: (bmap(i), 0, 1))],
        out_specs=[tok(d), tab, tab] if emit_tables else tok(d),
        out_shape=([jax.ShapeDtypeStruct((t, d), F32)] + [jax.ShapeDtypeStruct((n_exp, d), BF16)] * 2
                   if emit_tables else jax.ShapeDtypeStruct((t, d), F32)),
        scratch_shapes=[pltpu.VMEM((n_keys, tb, n_keys), BF16), pltpu.VMEM((tb, eb), BF16)],
        compiler_params=_params("arbitrary", "arbitrary"),
        name="peer_mix",
    )(h, i_sel, j_sel, g_sel, u_tab, v_tab, x, mods, gf.reshape(1, d), fmods, fmods)


def _token_path(x, mods, fmods, pos, tb, blocks_per_batch, attend, w, tables, emit_tables):
    t = x.shape[0]
    h = _prenorm(x, w["norm_mix_g"], mods, tb, blocks_per_batch)
    proj = _inproj(h, w["w_in"], pos, min(INPROJ_TOKEN_BLOCK, t))
    sbo, do = attend(proj)
    x2, h2_hi, h2_lo = _merge(sbo, do, x, w["w_out"], w["diff_norm_g"], w["norm_ffn_g"], mods,
                    w["lam_init"], w["heads"], tb, blocks_per_batch)
    i_sel, j_sel, g_sel = _route(h2_hi, h2_lo, w["wq_hi"], w["wq_lo"], w["keys_hi"], w["keys_lo"], tb)
    tb_peer = min(PEER_TOKEN_BLOCK, t)
    out = _peer(h2_hi, i_sel.T, j_sel.T, g_sel.T, tables[0], tables[1], x2, mods, fmods,
                w["norm_final_g"], tb_peer, blocks_per_batch * tb // tb_peer, emit_tables)
    return (out[0], proj, (out[1], out[2])) if emit_tables else (out, proj, tables)


def kernel(x_prompt, x_sample, c_prompt, c_sample, cache_sb_k, cache_sb_v, cache_diff_k, cache_diff_v, page_table, norm_mix_g, norm_ffn_g, w_ada, b_ada, w_in, diff_lambda, diff_norm_g, w_out, peer_w_q, peer_sub_keys, peer_u, peer_v, norm_final_g, w_ada_final, b_ada_final):
    batch, seq, d = x_prompt.shape
    nsamp, dec_seq, _ = x_sample.shape
    depth = w_ada.shape[0]
    assert depth == 1 and dec_seq == 1
    heads = cache_sb_k.shape[3]
    hd = cache_sb_k.shape[4]
    past_len = page_table.shape[1] * cache_sb_k.shape[2]
    l = 0
    lam_init = 0.8 - 0.6 * math.exp(-0.3 * l)

    nc = batch + nsamp
    c_all = jnp.concatenate([c_prompt, c_sample, jnp.zeros((-nc % 8, d), F32)], axis=0)
    mods = _adaln(c_all, w_ada[l], b_ada[l])
    fmods = _adaln(c_all, w_ada_final, b_ada_final)
    mods_p, mods_s = mods[:batch, None, :], mods[None, batch:nc, :]
    fmods_p, fmods_s = fmods[:batch, None, :], fmods[None, batch:nc, :]

    wq_hi, wq_lo = _split_bf16(peer_w_q[l])
    keys_hi, keys_lo = _split_bf16(peer_sub_keys[l])
    w = dict(norm_mix_g=norm_mix_g[l], norm_ffn_g=norm_ffn_g[l], w_in=w_in[l],
             w_out=w_out[l].astype(BF16), diff_norm_g=diff_norm_g[l], lam_init=lam_init, heads=heads,
             wq_hi=wq_hi, wq_lo=wq_lo, keys_hi=keys_hi, keys_lo=keys_lo,
             norm_final_g=norm_final_g)
    lam_p = diff_lambda[l]

    def attend_prompt(proj):
        return (_sb_prompt(proj, batch, heads, ATTN_BLOCK, SB_CHUNK),
                _diff_prompt(proj, lam_p, lam_init, batch, heads, ATTN_BLOCK))

    def attend_sample(proj):
        sb_q, _, _, d_q, d_k, d_v = proj
        pool, page = cache_sb_k.shape[1], cache_sb_k.shape[2]
        rows = lambda c: c.reshape(pool, page * heads, hd)
        dkt = jnp.transpose(cache_diff_k, (0, 1, 3, 4, 5, 2)).reshape(pool, heads * hd, page)
        return _sample_attn(sb_q, d_q, d_k, d_v, rows(cache_sb_k), rows(cache_sb_v), dkt,
                            rows(cache_diff_v), page_table, lam_p, lam_init, heads,
                            math.gcd(PAGES_PER_STEP, page_table.shape[1]))

    y_s, proj_s, tables = _token_path(x_sample.reshape(nsamp, d), mods_s, fmods_s,
                                      jnp.full((nsamp,), past_len, jnp.int32), nsamp, 1, attend_sample, w,
                                      (peer_u[l], peer_v[l]), True)
    tb_p = TOKEN_BLOCK
    y_p, proj_p, _ = _token_path(x_prompt.reshape(batch * seq, d), mods_p, fmods_p, jnp.arange(seq),
                                 tb_p, seq // tb_p, attend_prompt, w, tables, False)

    def rows(proj, n, t):
        sb_k, sb_v, d_k, d_v = proj[1], proj[2], proj[4], proj[5]
        return (sb_k.reshape(1, n, t, heads, hd), sb_v.reshape(1, n, t, heads, hd),
                d_k.reshape(1, n, t, heads, 2, hd // 2), d_v.reshape(1, n, t, heads, hd))

    kv_tb = min(LAYOUT_TOKEN_BLOCK, seq)
    sb_k, sb_v, d_v = (_head_rows(proj_p, s, heads, kv_tb).reshape(1, batch, seq, heads, hd)
                       for s in (1, 2, 5))
    d_k = _token_minor(proj_p, 4, batch, min(INPROJ_TOKEN_BLOCK, seq))
    d_k = jnp.transpose(d_k.reshape(batch, heads, 2, hd // 2, seq), (0, 4, 1, 2, 3))[None]

    return ((y_p.reshape(batch, seq, d), y_s.reshape(nsamp, dec_seq, d))
            + (sb_k, sb_v, d_k, d_v) + rows(proj_s, nsamp, dec_seq))
```
